```python
import jax, jax.numpy as jnp
from jax import lax
import numpy as np

D_MODEL = 1024
BATCH = 8
SEQ = 2048
DEPTH = 1

CHUNK = 64
N_META = 16
EPS = 1e-6

D_MIX = D_MODEL
M_HEADS = 4
M_HEAD_DIM = (D_MIX // 2) // M_HEADS
M_WIDTH = M_HEADS * M_HEAD_DIM
CONV_W = 4
F_HEADS = 8
F_HEAD_DIM = (D_MIX - M_WIDTH) // F_HEADS
F_WIDTH = F_HEADS * F_HEAD_DIM
Q_BLOCK = 128

PROJ_SIZES = (M_WIDTH, M_WIDTH, M_WIDTH, M_WIDTH, M_HEADS, M_HEADS, F_WIDTH, F_WIDTH, F_WIDTH, F_HEADS)
PROJ_DIM = 4 * M_WIDTH + 2 * M_HEADS + 3 * F_WIDTH + F_HEADS

PEER_HEADS = 8
N_KEYS = 128
N_EXPERTS = N_KEYS * N_KEYS
PEER_TOPK = 16
PEER_QDIM = 256
PEER_HALF = PEER_QDIM // 2
PEER_TOKEN_BLOCK = 128

kernel_name = "hybrid_mlstm_fox_peer_block"


def _rmsnorm(x, g):
    xf = x.astype(jnp.float32)
    y = xf * lax.rsqrt(jnp.mean(xf * xf, axis=-1, keepdims=True) + EPS)
    return (y * g.astype(jnp.float32)).astype(x.dtype)


def _proj_splits():
    out, acc = [], 0
    for s in PROJ_SIZES[:-1]:
        acc += s
        out.append(acc)
    return out


def _causal_dwconv(x, w):
    c = x.shape[-1]
    return lax.conv_general_dilated(
        x, w.astype(x.dtype)[:, None, :], window_strides=(1,),
        padding=[(CONV_W - 1, 0)], dimension_numbers=("NWC", "WIO", "NWC"),
        feature_group_count=c)


def _mlstm(q, k, v, o_pre, i_pre, f_pre, out_gain):
    f32 = jnp.float32
    B, L, _ = q.shape
    n_chunks = -(-L // CHUNK)
    pad = n_chunks * CHUNK - L

    def heads(t):
        t = jnp.pad(t.astype(f32), ((0, 0), (0, pad), (0, 0)))
        return t.reshape(B, n_chunks, CHUNK, M_HEADS, M_HEAD_DIM).transpose(1, 0, 3, 2, 4)

    def gates(t):
        t = jnp.pad(t.astype(f32), ((0, 0), (0, pad), (0, 0)))
        return t.reshape(B, n_chunks, CHUNK, M_HEADS).transpose(1, 0, 3, 2)

    qc = heads(q)
    kc = heads(k) * (M_HEAD_DIM ** -0.5)
    vc = heads(v)
    logi = gates(i_pre)
    logf = jax.nn.log_sigmoid(gates(f_pre))
    causal = jnp.asarray(np.tril(np.ones((CHUNK, CHUNK), dtype=bool)))

    def step(carry, inp):
        C, n, m = carry
        qb, kb, vb, li, lf = inp
        b = jnp.cumsum(lf, axis=-1)
        Dm = jnp.where(causal, b[..., :, None] - b[..., None, :] + li[..., None, :], -jnp.inf)
        m_inter = b + m[..., None]
        m_t = jnp.maximum(m_inter, jnp.max(Dm, axis=-1))
        S = jnp.einsum('bhtd,bhsd->bhts', qb, kb) * jnp.exp(Dm - m_t[..., None])
        w_inter = jnp.exp(m_inter - m_t)
        num = jnp.einsum('bhts,bhsd->bhtd', S, vb) + w_inter[..., None] * jnp.einsum('bhvk,bhtk->bhtv', C, qb)
        den = jnp.sum(S, axis=-1) + w_inter * jnp.einsum('bhk,bhtk->bht', n, qb)
        h = num / jnp.maximum(jnp.abs(den), jnp.exp(-m_t))[..., None]
        b_end = b[..., -1]
        g = b_end[..., None] - b + li
        m_new = jnp.maximum(b_end + m, jnp.max(g, axis=-1))
        decay = jnp.exp(b_end + m - m_new)
        wg = jnp.exp(g - m_new[..., None])
        C = decay[..., None, None] * C + jnp.einsum('bhs,bhsv,bhsk->bhvk', wg, vb, kb)
        n = decay[..., None] * n + jnp.einsum('bhs,bhsk->bhk', wg, kb)
        return (C, n, m_new), h

    init = (jnp.zeros((B, M_HEADS, M_HEAD_DIM, M_HEAD_DIM), f32),
            jnp.zeros((B, M_HEADS, M_HEAD_DIM), f32),
            jnp.zeros((B, M_HEADS), f32))
    _, h = lax.scan(step, init, (qc, kc, vc, logi, logf))
    h = h.transpose(1, 0, 3, 2, 4).reshape(B, n_chunks * CHUNK, M_HEADS, M_HEAD_DIM)[:, :L]
    h = _rmsnorm(h, out_gain.reshape(M_HEADS, M_HEAD_DIM)).reshape(B, L, M_WIDTH)
    h = h * jax.nn.sigmoid(o_pre.astype(f32))
    return h.astype(q.dtype)


def _forgetting_attention(q, k, v, f_pre, g_q, g_k):
    B, L, _ = q.shape

    def heads(t):
        return t.reshape(B, L, F_HEADS, F_HEAD_DIM).transpose(0, 2, 1, 3)

    qh = _rmsnorm(heads(q), g_q)
    kh = _rmsnorm(heads(k), g_k)
    vh = heads(v)
    c = jnp.cumsum(jax.nn.log_sigmoid(f_pre.astype(jnp.float32)), axis=1).transpose(0, 2, 1)
    scale = F_HEAD_DIM ** -0.5
    outs = []
    for start in range(0, L, Q_BLOCK):
        end = min(start + Q_BLOCK, L)
        logits = jnp.einsum('bhtd,bhsd->bhts', qh[:, :, start:end], kh[:, :, :end]).astype(jnp.float32) * scale
        logits = logits + c[:, :, start:end, None] - c[:, :, None, :end]
        mask = np.arange(start, end)[:, None] >= np.arange(end)[None, :]
        logits = jnp.where(mask, logits, -jnp.inf)
        p = jax.nn.softmax(logits, axis=-1).astype(vh.dtype)
        outs.append(jnp.einsum('bhts,bhsd->bhtd', p, vh[:, :, :end]))
    out = jnp.concatenate(outs, axis=2)
    return out.transpose(0, 2, 1, 3).reshape(B, L, F_WIDTH)


def _peer(x, w_query, sub_keys, u_emb, v_emb):
    B, L, D = x.shape
    T = B * L
    Tp = -(-T // PEER_TOKEN_BLOCK) * PEER_TOKEN_BLOCK
    xt = jnp.pad(x.reshape(T, D), ((0, Tp - T), (0, 0))).reshape(-1, PEER_TOKEN_BLOCK, D)
    keys32 = sub_keys.astype(jnp.float32)

    def block(xb):
        tb = xb.shape[0]
        qh = (xb @ w_query).astype(jnp.float32).reshape(tb, PEER_HEADS, 2, PEER_HALF)
        scores = jnp.einsum('thpc,hpnc->thpn', qh, keys32)
        val, idx = lax.top_k(scores, PEER_TOPK)
        cand = (val[:, :, 0, :, None] + val[:, :, 1, None, :]).reshape(tb, PEER_HEADS, PEER_TOPK * PEER_TOPK)
        cand_id = (idx[:, :, 0, :, None] * N_KEYS + idx[:, :, 1, None, :]).reshape(tb, PEER_HEADS, PEER_TOPK * PEER_TOPK)
        top_s, pos = lax.top_k(cand, PEER_TOPK)
        ids = jnp.take_along_axis(cand_id, pos, axis=-1).reshape(tb, PEER_HEADS * PEER_TOPK)
        gate = jax.nn.softmax(top_s, axis=-1).reshape(tb, PEER_HEADS * PEER_TOPK)
        u = u_emb[ids]
        act = jax.nn.gelu(jnp.einsum('ted,td->te', u, xb).astype(jnp.float32), approximate=False)
        w = (gate * act).astype(xb.dtype)
        return jnp.einsum('te,ted->td', w, v_emb[ids])

    out = lax.map(block, xt)
    return out.reshape(Tp, D)[:T].reshape(B, L, D)


def setup_inputs(seed: int = 0) -> dict:
    key = jax.random.key(seed)
    ks = jax.random.split(key, 20)
    f32 = jnp.float32
    nrm = lambda k, shape, s: jax.random.normal(k, shape, f32) * s
    fbias_m = jnp.broadcast_to(jnp.linspace(3.0, 6.0, M_HEADS, dtype=f32), (DEPTH, M_HEADS))
    fbias_f = jnp.broadcast_to(jnp.linspace(2.0, 6.0, F_HEADS, dtype=f32), (DEPTH, F_HEADS))
    return {
        "x": nrm(ks[0], (BATCH, SEQ, D_MODEL), 1.0),
        "meta_tokens": nrm(ks[1], (N_META, D_MODEL), 1.0),
        "norm_mix": 1.0 + nrm(ks[2], (DEPTH, D_MODEL), 0.02),
        "w_in": nrm(ks[3], (DEPTH, D_MODEL, PROJ_DIM), D_MODEL ** -0.5),
        "conv_qk": nrm(ks[4], (DEPTH, CONV_W, 2 * M_WIDTH), CONV_W ** -0.5),
        "b_igate": nrm(ks[5], (DEPTH, M_HEADS), 0.1),
        "b_fgate_m": fbias_m + nrm(ks[6], (DEPTH, M_HEADS), 0.1),
        "m_out_norm": 1.0 + nrm(ks[7], (DEPTH, M_WIDTH), 0.02),
        "b_fgate_f": fbias_f + nrm(ks[8], (DEPTH, F_HEADS), 0.1),
        "f_q_norm": 1.0 + nrm(ks[9], (DEPTH, F_HEAD_DIM), 0.02),
        "f_k_norm": 1.0 + nrm(ks[10], (DEPTH, F_HEAD_DIM), 0.02),
        "w_out": nrm(ks[11], (DEPTH, D_MIX, D_MODEL), D_MIX ** -0.5),
        "norm_ffn": 1.0 + nrm(ks[12], (DEPTH, D_MODEL), 0.02),
        "peer_query": nrm(ks[13], (DEPTH, D_MODEL, PEER_HEADS * PEER_QDIM), D_MODEL ** -0.5),
        "peer_sub_keys": nrm(ks[14], (DEPTH, PEER_HEADS, 2, N_KEYS, PEER_HALF), PEER_HALF ** -0.5),
        "peer_u": nrm(ks[15], (DEPTH, N_EXPERTS, D_MODEL), D_MODEL ** -0.5),
        "peer_v": nrm(ks[16], (DEPTH, N_EXPERTS, D_MODEL), 0.1),
    }


def reference(x, meta_tokens, norm_mix, w_in, conv_qk, b_igate, b_fgate_m, m_out_norm,
              b_fgate_f, f_q_norm, f_k_norm, w_out, norm_ffn, peer_query, peer_sub_keys,
              peer_u, peer_v):
    B = x.shape[0]
    meta = jnp.broadcast_to(meta_tokens.astype(x.dtype)[None], (B, N_META, D_MODEL))
    h_res = jnp.concatenate([meta, x], axis=1)
    splits = _proj_splits()
    for layer in range(DEPTH):
        hn = _rmsnorm(h_res, norm_mix[layer])
        z = hn @ w_in[layer]
        mq, mk, mv, mo, mi, mf, fq, fk, fv, ff = jnp.split(z, splits, axis=-1)
        qk = jax.nn.silu(_causal_dwconv(jnp.concatenate([mq, mk], axis=-1), conv_qk[layer]))
        mq, mk = jnp.split(qk, 2, axis=-1)
        y_m = _mlstm(mq, mk, mv, mo, mi + b_igate[layer], mf + b_fgate_m[layer], m_out_norm[layer])
        y_f = _forgetting_attention(fq, fk, fv, ff + b_fgate_f[layer], f_q_norm[layer], f_k_norm[layer])
        h_res = h_res + jnp.concatenate([y_m, y_f], axis=-1) @ w_out[layer]
        if layer == DEPTH - 1:
            h_res = h_res[:, N_META:]
        h_res = h_res + _peer(_rmsnorm(h_res, norm_ffn[layer]), peer_query[layer],
                              peer_sub_keys[layer], peer_u[layer], peer_v[layer])
    return h_res
```

```python
import functools

import jax
import jax.numpy as jnp
from jax import lax
from jax.experimental import pallas as pl
from jax.experimental.pallas import tpu as pltpu

F32 = jnp.float32
BF16 = jnp.bfloat16

EPS = 1e-6
N_META = 16
M_HEADS = 4
M_HEAD_DIM = 128
M_WIDTH = M_HEADS * M_HEAD_DIM
CONV_W = 4
F_HEADS = 8
F_HEAD_DIM = 64
F_WIDTH = F_HEADS * F_HEAD_DIM
PEER_HEADS = 8
N_KEYS = 128
PEER_TOPK = 16
PEER_HALF = 128

SEQ_BLOCK = 128
GATE_LANES = 128
NEG_BIG = -1e30
VMEM_LIMIT = 56 * 1024 * 1024

_NT = (((1,), (1,)), ((), ()))
_TN = (((0,), (0,)), ((), ()))


def _params(*sem):
    return pltpu.CompilerParams(dimension_semantics=sem, vmem_limit_bytes=VMEM_LIMIT)


def _split_bf16(a):
    hi = a.astype(BF16)
    lo = (a - hi.astype(F32)).astype(BF16)
    return hi, lo


def _dot(a, b):
    return jnp.dot(a, b, preferred_element_type=F32)


def _dot3(ah, al, bh, bl, dims=None):
    if dims is None:
        f = _dot
    else:
        f = lambda x, y: lax.dot_general(x, y, dims, preferred_element_type=F32)
    return f(ah, bh) + (f(ah, bl) + f(al, bh))


def _inproj_kernel(h_ref, g_ref, wm_ref, wf_ref, wgh_ref, wgl_ref, zm_ref, zf_ref, zg_ref):
    h = h_ref[...]
    hn = h * lax.rsqrt(jnp.mean(h * h, axis=-1, keepdims=True) + EPS) * g_ref[...]
    hb, hl = _split_bf16(hn)
    zm_ref[...] = _dot(hb, wm_ref[...])
    zf_ref[...] = _dot(hb, wf_ref[...])
    zg_ref[...] = _dot3(hb, hl, wgh_ref[...], wgl_ref[...])


def _inproj(h2d, g, wm, wf, wgh, wgl, tm):
    n, d = h2d.shape
    full = lambda a: pl.BlockSpec(a.shape, lambda i: (0, 0))
    return pl.pallas_call(
        _inproj_kernel,
        grid=(n // tm,),
        in_specs=[pl.BlockSpec((tm, d), lambda i: (i, 0)), full(g), full(wm), full(wf), full(wgh), full(wgl)],
        out_specs=[pl.BlockSpec((tm, wm.shape[1]), lambda i: (i, 0)),
                   pl.BlockSpec((tm, wf.shape[1]), lambda i: (i, 0)),
                   pl.BlockSpec((tm, GATE_LANES), lambda i: (i, 0))],
        out_shape=[jax.ShapeDtypeStruct((n, wm.shape[1]), F32),
                   jax.ShapeDtypeStruct((n, wf.shape[1]), F32),
                   jax.ShapeDtypeStruct((n, GATE_LANES), F32)],
        compiler_params=_params("parallel"),
        name="inproj",
    )(h2d, g, wm, wf, wgh, wgl)


def _log_sigmoid(x):
    return jnp.minimum(x, 0.0) - jnp.log1p(jnp.exp(-jnp.abs(x)))


def _gates_kernel(zg_ref, bias_ref, gcol_ref, grow_ref):
    lp = zg_ref.shape[0]
    t = SEQ_BLOCK
    r = lax.broadcasted_iota(jnp.int32, (t, t), 0)
    c = lax.broadcasted_iota(jnp.int32, (t, t), 1)
    tri = jnp.where(r >= c, 1.0, 0.0).astype(BF16)
    lane = lax.broadcasted_iota(jnp.int32, (t, GATE_LANES), 1)
    carry = jnp.zeros((1, GATE_LANES), F32)
    for i in range(lp // t):
        pre = zg_ref[i * t:(i + 1) * t, :] + bias_ref[...]
        ls = _log_sigmoid(pre)
        l0 = ls.astype(BF16)
        r1 = ls - l0.astype(F32)
        l1 = r1.astype(BF16)
        l2 = (r1 - l1.astype(F32)).astype(BF16)
        cs = (_dot(tri, l0) + (_dot(tri, l1) + _dot(tri, l2))) + carry
        carry = cs[t - 1:t, :]
        out = jnp.where(lane < M_HEADS, pre, cs)
        gcol_ref[i * t:(i + 1) * t, :] = out
        grow_ref[:, i * t:(i + 1) * t] = out.T[:16, :]


def _gates(zg, bias):
    b, lp, _ = zg.shape
    return pl.pallas_call(
        _gates_kernel,
        grid=(b,),
        in_specs=[pl.BlockSpec((None, lp, GATE_LANES), lambda i: (i, 0, 0)),
                  pl.BlockSpec((1, GATE_LANES), lambda i: (0, 0))],
        out_specs=[pl.BlockSpec((None, lp, GATE_LANES), lambda i: (i, 0, 0)),
                   pl.BlockSpec((None, 16, lp), lambda i: (i, 0, 0))],
        out_shape=[jax.ShapeDtypeStruct((b, lp, GATE_LANES), F32),
                   jax.ShapeDtypeStruct((b, 16, lp), F32)],
        compiler_params=_params("parallel"),
        name="gates",
    )(zg, bias)


def _mlstm_kernel(zm_ref, gcol_ref, grow_ref, convw_ref, gain_ref, y_ref,
                  prevx_ref, ct_ref, n_ref, m_ref, pc_ref):
    j = pl.program_id(1)
    t = SEQ_BLOCK
    w2 = 2 * M_WIDTH

    @pl.when(j == 0)
    def _():
        prevx_ref[...] = jnp.zeros_like(prevx_ref)
        ct_ref[...] = jnp.zeros_like(ct_ref)
        n_ref[...] = jnp.zeros_like(n_ref)
        m_ref[...] = jnp.zeros_like(m_ref)
        pc_ref[...] = jnp.zeros_like(pc_ref)

    xqk = zm_ref[:, :w2]
    prev = prevx_ref[...]
    rowi = lax.broadcasted_iota(jnp.int32, (t, w2), 0)
    acc = xqk * convw_ref[CONV_W - 1:CONV_W, :]
    for s in range(1, CONV_W):
        shifted = jnp.where(rowi < s, pltpu.roll(prev, s, 0), pltpu.roll(xqk, s, 0))
        acc = acc + shifted * convw_ref[CONV_W - 1 - s:CONV_W - s, :]
    prevx_ref[...] = xqk
    qk = acc * jax.nn.sigmoid(acc)

    gcol = gcol_ref[...]
    grow = grow_ref[...]
    rr = lax.broadcasted_iota(jnp.int32, (t, t), 0)
    cc = lax.broadcasted_iota(jnp.int32, (t, t), 1)
    causal = rr >= cc
    scale = M_HEAD_DIM ** -0.5

    for h in range(M_HEADS):
        sl = slice(h * M_HEAD_DIM, (h + 1) * M_HEAD_DIM)
        q = qk[:, sl]
        k = qk[:, M_WIDTH + h * M_HEAD_DIM:M_WIDTH + (h + 1) * M_HEAD_DIM] * scale
        v = zm_ref[:, w2 + h * M_HEAD_DIM:w2 + (h + 1) * M_HEAD_DIM]
        o_pre = zm_ref[:, w2 + M_WIDTH + h * M_HEAD_DIM:w2 + M_WIDTH + (h + 1) * M_HEAD_DIM]
        qb, kb, vb = q.astype(BF16), k.astype(BF16), v.astype(BF16)

        li_c = gcol[:, h:h + 1]
        li_r = grow[h:h + 1, :]
        prev_cum = pc_ref[0:1, M_HEADS + h:M_HEADS + h + 1]
        b_c = gcol[:, M_HEADS + h:M_HEADS + h + 1] - prev_cum
        b_r = grow[M_HEADS + h:M_HEADS + h + 1, :] - prev_cum
        m_prev = m_ref[h, 0:1, 0:1]
        ct = ct_ref[h]
        n_row = n_ref[h, 0:1, :]

        dm = jnp.where(causal, (b_c - b_r) + li_r, -jnp.inf)
        m_inter = b_c + m_prev
        m_t = jnp.maximum(m_inter, jnp.max(dm, axis=1, keepdims=True))
        s_mat = lax.dot_general(qb, kb, _NT, preferred_element_type=F32) * jnp.exp(dm - m_t)
        w_inter = jnp.exp(m_inter - m_t)
        num = _dot(s_mat.astype(BF16), vb) + w_inter * _dot(qb, ct.astype(BF16))
        den = jnp.sum(s_mat, axis=1, keepdims=True) + w_inter * jnp.sum(q * n_row, axis=1, keepdims=True)
        hh = num / jnp.maximum(jnp.abs(den), jnp.exp(-m_t))

        b_end = b_c[t - 1:t, :]
        g_c = (b_end - b_c) + li_c
        g_r = (b_end - b_r) + li_r
        m_new = jnp.maximum(b_end + m_prev, jnp.max(g_r, axis=1, keepdims=True))
        decay = jnp.exp(b_end + m_prev - m_new)
        kw = k * jnp.exp(g_c - m_new)
        ct_ref[h] = decay * ct + lax.dot_general(kw.astype(BF16), vb, _TN, preferred_element_type=F32)
        n_ref[h] = jnp.broadcast_to(decay * n_row + jnp.sum(kw, axis=0, keepdims=True), (8, M_HEAD_DIM))
        m_ref[h] = jnp.broadcast_to(m_new, (8, 128))

        hn = hh * lax.rsqrt(jnp.mean(hh * hh, axis=-1, keepdims=True) + EPS) * gain_ref[:, sl]
        y_ref[:, sl] = (hn * jax.nn.sigmoid(o_pre)).astype(y_ref.dtype)

    pc_ref[...] = jnp.broadcast_to(gcol[t - 1:t, :], (8, GATE_LANES))


def _mlstm(zm, gcol, grow, convw, gain):
    b, lp, wz = zm.shape
    t = SEQ_BLOCK
    nb = lp // t
    return pl.pallas_call(
        _mlstm_kernel,
        grid=(b, nb),
        in_specs=[pl.BlockSpec((None, t, wz), lambda i, j: (i, j, 0)),
                  pl.BlockSpec((None, t, GATE_LANES), lambda i, j: (i, j, 0)),
                  pl.BlockSpec((None, 16, t), lambda i, j: (i, 0, j)),
                  pl.BlockSpec(convw.shape, lambda i, j: (0, 0)),
                  pl.BlockSpec(gain.shape, lambda i, j: (0, 0))],
        out_specs=pl.BlockSpec((None, t, M_WIDTH), lambda i, j: (i, jnp.maximum(j - 1, 0), 0)),
        out_shape=jax.ShapeDtypeStruct((b, lp - t, M_WIDTH), BF16),
        scratch_shapes=[pltpu.VMEM((t, 2 * M_WIDTH), F32),
                        pltpu.VMEM((M_HEADS, M_HEAD_DIM, M_HEAD_DIM), F32),
                        pltpu.VMEM((M_HEADS, 8, M_HEAD_DIM), F32),
                        pltpu.VMEM((M_HEADS, 8, 128), F32),
                        pltpu.VMEM((8, GATE_LANES), F32)],
        compiler_params=_params("parallel", "arbitrary"),
        name="mlstm",
    )(zm, gcol, grow, convw, gain)


def _headnorm(x, g):
    return x * lax.rsqrt(jnp.mean(x * x, axis=-1, keepdims=True) + EPS) * g


def _fox_kernel(first_real, q_ref, k_ref, v_ref, gcol_ref, grow_ref, gq_ref, gk_ref, y_ref, kn_ref, vb_ref):
    qi = pl.program_id(1) + 1
    t = SEQ_BLOCK
    d = F_HEAD_DIM

    @pl.when(qi == 1)
    def _():
        for h in range(F_HEADS):
            sl = slice(h * d, (h + 1) * d)
            kn_ref[:, sl] = _headnorm(k_ref[:, sl], gk_ref[...]).astype(BF16)
        vb_ref[...] = v_ref[...].astype(BF16)

    rr = lax.broadcasted_iota(jnp.int32, (t, t), 0)
    cc = lax.broadcasted_iota(jnp.int32, (t, t), 1)
    causal = rr >= cc
    not_pad = cc >= first_real
    gcol = gcol_ref[...]
    scale = d ** -0.5

    for h in range(F_HEADS):
        sl = slice(h * d, (h + 1) * d)
        lane = 2 * M_HEADS + h
        qn = (_headnorm(q_ref[:, sl], gq_ref[...]) * scale).astype(BF16)
        cq = gcol[:, lane:lane + 1]

        def block(jb, carry, mask):
            m, l, acc = carry
            start = pl.multiple_of(jb * t, t)
            kb = kn_ref[pl.ds(start, t), sl]
            vb = vb_ref[pl.ds(start, t), sl]
            ck = grow_ref[lane:lane + 1, pl.ds(start, t)]
            s = lax.dot_general(qn, kb, _NT, preferred_element_type=F32) + (cq - ck)
            if mask is not None:
                s = jnp.where(mask, s, NEG_BIG)
            m_new = jnp.maximum(m, jnp.max(s, axis=1, keepdims=True))
            p = jnp.exp(s - m_new)
            alpha = jnp.exp(m - m_new)
            l = alpha * l + jnp.sum(p, axis=1, keepdims=True)
            acc = alpha * acc + _dot(p.astype(BF16), vb)
            return m_new, l, acc

        carry = (jnp.full((t, 1), NEG_BIG, F32), jnp.zeros((t, 1), F32), jnp.zeros((t, d), F32))
        carry = block(0, carry, not_pad)
        carry = lax.fori_loop(1, qi, lambda jb, cr: block(jb, cr, None), carry)
        m, l, acc = block(qi, carry, causal)
        y_ref[:, sl] = (acc / l).astype(y_ref.dtype)


def _fox(zf, gcol, grow, gq, gk, first_real):
    b, lp, _ = zf.shape
    t = SEQ_BLOCK
    nb = lp // t
    return pl.pallas_call(
        functools.partial(_fox_kernel, first_real),
        grid=(b, nb - 1),
        in_specs=[pl.BlockSpec((None, t, F_WIDTH), lambda i, j: (i, j + 1, 0)),
                  pl.BlockSpec((None, lp, F_WIDTH), lambda i, j: (i, 0, 1)),
                  pl.BlockSpec((None, lp, F_WIDTH), lambda i, j: (i, 0, 2)),
                  pl.BlockSpec((None, t, GATE_LANES), lambda i, j: (i, j + 1, 0)),
                  pl.BlockSpec((None, 16, lp), lambda i, j: (i, 0, 0)),
                  pl.BlockSpec(gq.shape, lambda i, j: (0, 0)),
                  pl.BlockSpec(gk.shape, lambda i, j: (0, 0))],
        out_specs=pl.BlockSpec((None, t, F_WIDTH), lambda i, j: (i, j, 0)),
        out_shape=jax.ShapeDtypeStruct((b, lp - t, F_WIDTH), BF16),
        scratch_shapes=[pltpu.VMEM((lp, F_WIDTH), BF16), pltpu.VMEM((lp, F_WIDTH), BF16)],
        compiler_params=_params("parallel", "arbitrary"),
        name="fox",
    )(zf, zf, zf, gcol, grow, gq, gk)


def _outproj_kernel(ym_ref, yf_ref, x_ref, wm_ref, wf_ref, g_ref, h2_ref, xh_ref, xl_ref):
    h2 = x_ref[...] + (_dot(ym_ref[...], wm_ref[...]) + _dot(yf_ref[...], wf_ref[...]))
    h2_ref[...] = h2
    xn = h2 * lax.rsqrt(jnp.mean(h2 * h2, axis=-1, keepdims=True) + EPS) * g_ref[...]
    hi, lo = _split_bf16(xn)
    xh_ref[...] = hi
    xl_ref[...] = lo


def _outproj(ym, yf, x2d, wm, wf, g, tm):
    n, d = x2d.shape
    row = lambda w: pl.BlockSpec((tm, w), lambda i: (i, 0))
    full = lambda a: pl.BlockSpec(a.shape, lambda i: (0, 0))
    return pl.pallas_call(
        _outproj_kernel,
        grid=(n // tm,),
        in_specs=[row(ym.shape[1]), row(yf.shape[1]), row(d), full(wm), full(wf), full(g)],
        out_specs=[row(d), row(d), row(d)],
        out_shape=[jax.ShapeDtypeStruct((n, d), F32),
                   jax.ShapeDtypeStruct((n, d), BF16),
                   jax.ShapeDtypeStruct((n, d), BF16)],
        compiler_params=_params("parallel"),
        name="outproj",
    )(ym, yf, x2d, wm, wf, g)


_TOP = PEER_TOPK + 1
_TOP_ROWS = 24


def _top_values(s):
    tb = s.shape[1]
    rank = lax.broadcasted_iota(jnp.int32, (_TOP_ROWS, tb), 0).astype(F32)
    vals = jnp.full((_TOP_ROWS, tb), -jnp.inf, F32)
    cnt = jnp.zeros((1, tb), F32)
    work = s
    for _ in range(_TOP):
        m = jnp.max(work, axis=0, keepdims=True)
        eq = work == m
        k = jnp.sum(jnp.where(eq, 1.0, 0.0), axis=0, keepdims=True)
        vals = jnp.where((rank >= cnt) & (rank < cnt + k), m, vals)
        cnt = cnt + k
        work = jnp.where(eq, -jnp.inf, work)
    return vals


def _pair_threshold(a, b):
    k = PEER_TOPK
    slabs = [a[0:1, :] + b[0:_TOP_ROWS, :]]
    for r in range(1, 8):
        slabs.append(a[r:r + 1, :] + b[0:8, :])
    slabs.append(a[8:_TOP_ROWS, :] + b[0:1, :])
    work = jnp.concatenate(slabs, axis=0)
    top = a[0:1, :] + b[0:1, :]
    cnt = jnp.zeros_like(top)
    v_k = top
    v_k1 = top
    z = jnp.zeros_like(top)
    for _ in range(_TOP):
        m = jnp.max(work, axis=0, keepdims=True)
        eq = work == m
        n_eq = jnp.sum(jnp.where(eq, 1.0, 0.0), axis=0, keepdims=True)
        take = jnp.clip(k - cnt, 0.0, n_eq)
        z = z + take * jnp.exp(m - top)
        v_k = jnp.where(cnt < k, m, v_k)
        v_k1 = jnp.where(cnt < k + 1, m, v_k1)
        cnt = cnt + n_eq
        work = jnp.where(eq, -jnp.inf, work)
    return 0.5 * (v_k + v_k1), top, z


def _peer_route_kernel(xh_ref, xl_ref, wqh_ref, wql_ref, kh_ref, kl_ref, s2_ref, e2_ref, c_ref, e1_ref):
    q = _dot3(xh_ref[...], xl_ref[...], wqh_ref[...], wql_ref[...])
    for h in range(PEER_HEADS):
        st = []
        for p in range(2):
            i = 2 * h + p
            qh, ql = _split_bf16(q[:, i * PEER_HALF:(i + 1) * PEER_HALF])
            st.append(_dot3(kh_ref[i], kl_ref[i], qh, ql, _NT))
        a = _top_values(st[0])
        b = _top_values(st[1])
        thr, top, z = _pair_threshold(a, b)
        s2_ref[h] = st[1]
        e2_ref[h] = jnp.exp(st[1] - b[0:1, :])
        c_ref[h] = thr - st[0]
        e1_ref[h] = jnp.exp(st[0] - a[0:1, :]) * (1.0 / z)


def _peer_route(xh, xl, wqh, wql, kh, kl, tb):
    n, d = xh.shape
    row = pl.BlockSpec((tb, d), lambda i: (i, 0))
    full2 = lambda a: pl.BlockSpec(a.shape, lambda i: (0, 0))
    full3 = lambda a: pl.BlockSpec(a.shape, lambda i: (0, 0, 0))
    out = pl.BlockSpec((PEER_HEADS, N_KEYS, tb), lambda i: (0, 0, i))
    shape = jax.ShapeDtypeStruct((PEER_HEADS, N_KEYS, n), F32)
    return pl.pallas_call(
        _peer_route_kernel,
        grid=(n // tb,),
        in_specs=[row, row, full2(wqh), full2(wql), full3(kh), full3(kl)],
        out_specs=[out, out, out, out],
        out_shape=[shape, shape, shape, shape],
        compiler_params=_params("parallel"),
        name="peer_route",
    )(xh, xl, wqh, wql, kh, kl)


def _gelu(x):
    return 0.5 * x * (1.0 + lax.erf(x * (2.0 ** -0.5)))


def _peer_expert_kernel(ib, xn_ref, u_ref, vt_ref, s2_ref, e2_ref, c_ref, e1_ref, h2_ref, o_ref,
                        acc_ref, w_ref):
    e = pl.program_id(1)
    tb = xn_ref.shape[0]

    @pl.when(e == 0)
    def _():
        acc_ref[...] = jnp.zeros_like(acc_ref)

    act = lax.dot_general(u_ref[...], xn_ref[...], _NT, preferred_element_type=F32)
    for ii in range(ib):
        rows = slice(ii * N_KEYS, (ii + 1) * N_KEYS)
        for tc in range(tb // 128):
            cols = slice(tc * 128, (tc + 1) * 128)
            gate = jnp.zeros((N_KEYS, 128), F32)
            for h in range(PEER_HEADS):
                sel = jnp.where(s2_ref[h, :, cols] >= c_ref[h, ii:ii + 1, cols], e2_ref[h, :, cols], 0.0)
                gate = gate + e1_ref[h, ii:ii + 1, cols] * sel
            w_ref[rows, cols] = (gate * _gelu(act[rows, cols])).astype(BF16)
    acc_ref[...] += _dot(vt_ref[...], w_ref[...])

    @pl.when(e == pl.num_programs(1) - 1)
    def _():
        o_ref[...] = h2_ref[...] + acc_ref[...].T


def _peer_experts(xn, u, vt, s2, e2, c, e1, h2, tb, ib):
    n, d = xn.shape
    ne = u.shape[0]
    eb = ib * N_KEYS
    gate_full = pl.BlockSpec((PEER_HEADS, N_KEYS, tb), lambda t, e: (0, 0, t))
    gate_rows = pl.BlockSpec((PEER_HEADS, ib, tb), lambda t, e: (0, e, t))
    return pl.pallas_call(
        functools.partial(_peer_expert_kernel, ib),
        grid=(n // tb, ne // eb),
        in_specs=[pl.BlockSpec((tb, d), lambda t, e: (t, 0)),
                  pl.BlockSpec((eb, d), lambda t, e: (e, 0)),
                  pl.BlockSpec((d, eb), lambda t, e: (0, e)),
                  gate_full, gate_full, gate_rows, gate_rows,
                  pl.BlockSpec((tb, d), lambda t, e: (t, 0))],
        out_specs=pl.BlockSpec((tb, d), lambda t, e: (t, 0)),
        out_shape=jax.ShapeDtypeStruct((n, d), F32),
        scratch_shapes=[pltpu.VMEM((d, tb), F32), pltpu.VMEM((eb, tb), BF16)],
        compiler_params=_params("parallel", "arbitrary"),
        name="peer_experts",
    )(xn, u, vt, s2, e2, c, e1, h2)


def _tile(n, prefer):
    for t in prefer:
        if n % t == 0:
            return t
    raise ValueError(f"no tile for {n}")


def kernel(x, meta_tokens, norm_mix, w_in, conv_qk, b_igate, b_fgate_m, m_out_norm, b_fgate_f,
           f_q_norm, f_k_norm, w_out, norm_ffn, peer_query, peer_sub_keys, peer_u, peer_v):
    assert w_in.shape[0] == 1, "single-layer block"
    b, seq, d = x.shape
    t = SEQ_BLOCK
    assert seq % t == 0 and N_META <= t
    first_real = t - N_META
    lp = seq + t

    meta = jnp.broadcast_to(meta_tokens.astype(x.dtype)[None], (b, N_META, d))
    hpad = jnp.concatenate([jnp.zeros((b, first_real, d), x.dtype), meta, x], axis=1).reshape(b * lp, d)
    w = w_in[0]
    o = 0
    cols = {}
    for name, size in (("mq", M_WIDTH), ("mk", M_WIDTH), ("mv", M_WIDTH), ("mo", M_WIDTH), ("mi", M_HEADS),
                       ("mf", M_HEADS), ("fq", F_WIDTH), ("fk", F_WIDTH), ("fv", F_WIDTH), ("ff", F_HEADS)):
        cols[name] = w[:, o:o + size]
        o += size
    w_m = jnp.concatenate([cols["mq"], cols["mk"], cols["mv"], cols["mo"]], axis=1).astype(BF16)
    w_f = jnp.concatenate([cols["fq"], cols["fk"], cols["fv"]], axis=1).astype(BF16)
    n_gate = 2 * M_HEADS + F_HEADS
    w_g = jnp.concatenate([cols["mi"], cols["mf"], cols["ff"], jnp.zeros((d, GATE_LANES - n_gate), F32)], axis=1)
    w_gh, w_gl = _split_bf16(w_g)
    gate_bias = jnp.concatenate([b_igate[0], b_fgate_m[0], b_fgate_f[0],
                                 jnp.zeros((GATE_LANES - n_gate,), F32)]).reshape(1, GATE_LANES)

    zm, zf, zg = _inproj(hpad, norm_mix[0].reshape(1, d), w_m, w_f, w_gh, w_gl, _tile(b * lp, (512, 256, 128)))
    gcol, grow = _gates(zg.reshape(b, lp, GATE_LANES), gate_bias)
    y_m = _mlstm(zm.reshape(b, lp, 4 * M_WIDTH), gcol, grow, conv_qk[0], m_out_norm[0].reshape(1, M_WIDTH))
    y_f = _fox(zf.reshape(b, lp, 3 * F_WIDTH), gcol, grow, f_q_norm[0].reshape(1, F_HEAD_DIM),
               f_k_norm[0].reshape(1, F_HEAD_DIM), first_real)
    n = b * seq
    wo = w_out[0].astype(BF16)
    h2, xh, xl = _outproj(y_m.reshape(n, M_WIDTH), y_f.reshape(n, F_WIDTH), x.reshape(n, d),
                          wo[:M_WIDTH], wo[M_WIDTH:], norm_ffn[0].reshape(1, d), _tile(n, (512, 256, 128)))

    wqh, wql = _split_bf16(peer_query[0])
    kh, kl = _split_bf16(peer_sub_keys[0].reshape(2 * PEER_HEADS, N_KEYS, PEER_HALF))
    s2, e2, c, e1 = _peer_route(xh, xl, wqh, wql, kh, kl, _tile(n, (256, 128)))
    u = peer_u[0].astype(BF16)
    vt = peer_v[0].astype(BF16).T
    out = _peer_experts(xh, u, vt, s2, e2, c, e1, h2, _tile(n, (512, 256, 128)), 8)
    return out.reshape(b, seq, d)
```

```python
import functools

import jax
import jax.numpy as jnp
from jax import lax
from jax.experimental import pallas as pl
from jax.experimental.pallas import tpu as pltpu

F32 = jnp.float32
BF16 = jnp.bfloat16

EPS = 1e-6
N_META = 16
M_HEADS = 4
M_HEAD_DIM = 128
M_WIDTH = M_HEADS * M_HEAD_DIM
CONV_W = 4
F_HEADS = 8
F_HEAD_DIM = 64
F_WIDTH = F_HEADS * F_HEAD_DIM
PEER_HEADS = 8
N_KEYS = 128
PEER_TOPK = 16
PEER_HALF = 128

SEQ_BLOCK = 128
FOX_BLOCK = 256
GATE_LANES = 128
NEG_BIG = -1e30
VMEM_LIMIT = 56 * 1024 * 1024

_NT = (((1,), (1,)), ((), ()))
_TN = (((0,), (0,)), ((), ()))


def _params(*sem):
    return pltpu.CompilerParams(dimension_semantics=sem, vmem_limit_bytes=VMEM_LIMIT)


def _split_bf16(a):
    hi = a.astype(BF16)
    lo = (a - hi.astype(F32)).astype(BF16)
    return hi, lo


def _dot(a, b):
    return jnp.dot(a, b, preferred_element_type=F32)


def _dot3(ah, al, bh, bl, dims=None):
    if dims is None:
        f = _dot
    else:
        f = lambda x, y: lax.dot_general(x, y, dims, preferred_element_type=F32)
    return f(ah, bh) + (f(ah, bl) + f(al, bh))


def _inproj_kernel(h_ref, g_ref, wm_ref, wf_ref, wgh_ref, wgl_ref, zm_ref, zf_ref, zg_ref):
    h = h_ref[...]
    hn = h * lax.rsqrt(jnp.mean(h * h, axis=-1, keepdims=True) + EPS) * g_ref[...]
    hb, hl = _split_bf16(hn)
    zm_ref[...] = _dot(hb, wm_ref[...])
    zf_ref[...] = _dot(hb, wf_ref[...])
    zg_ref[...] = _dot3(hb, hl, wgh_ref[...], wgl_ref[...])


def _inproj(h2d, g, wm, wf, wgh, wgl, tm):
    n, d = h2d.shape
    full = lambda a: pl.BlockSpec(a.shape, lambda i: (0, 0))
    return pl.pallas_call(
        _inproj_kernel,
        grid=(n // tm,),
        in_specs=[pl.BlockSpec((tm, d), lambda i: (i, 0)), full(g), full(wm), full(wf), full(wgh), full(wgl)],
        out_specs=[pl.BlockSpec((tm, wm.shape[1]), lambda i: (i, 0)),
                   pl.BlockSpec((tm, wf.shape[1]), lambda i: (i, 0)),
                   pl.BlockSpec((tm, GATE_LANES), lambda i: (i, 0))],
        out_shape=[jax.ShapeDtypeStruct((n, wm.shape[1]), F32),
                   jax.ShapeDtypeStruct((n, wf.shape[1]), F32),
                   jax.ShapeDtypeStruct((n, GATE_LANES), F32)],
        compiler_params=_params("parallel"),
        name="inproj",
    )(h2d, g, wm, wf, wgh, wgl)


def _log_sigmoid(x):
    return jnp.minimum(x, 0.0) - jnp.log1p(jnp.exp(-jnp.abs(x)))


def _gates_kernel(zg_ref, bias_ref, gcol_ref, grow_ref):
    lp = zg_ref.shape[0]
    t = SEQ_BLOCK
    r = lax.broadcasted_iota(jnp.int32, (t, t), 0)
    c = lax.broadcasted_iota(jnp.int32, (t, t), 1)
    tri = jnp.where(r >= c, 1.0, 0.0).astype(BF16)
    lane = lax.broadcasted_iota(jnp.int32, (t, GATE_LANES), 1)
    carry = jnp.zeros((1, GATE_LANES), F32)
    for i in range(lp // t):
        pre = zg_ref[i * t:(i + 1) * t, :] + bias_ref[...]
        ls = _log_sigmoid(pre)
        l0 = ls.astype(BF16)
        r1 = ls - l0.astype(F32)
        l1 = r1.astype(BF16)
        l2 = (r1 - l1.astype(F32)).astype(BF16)
        cs = (_dot(tri, l0) + (_dot(tri, l1) + _dot(tri, l2))) + carry
        carry = cs[t - 1:t, :]
        out = jnp.where(lane < M_HEADS, pre, cs)
        gcol_ref[i * t:(i + 1) * t, :] = out
        grow_ref[:, i * t:(i + 1) * t] = out.T[:16, :]


def _gates(zg, bias):
    b, lp, _ = zg.shape
    return pl.pallas_call(
        _gates_kernel,
        grid=(b,),
        in_specs=[pl.BlockSpec((None, lp, GATE_LANES), lambda i: (i, 0, 0)),
                  pl.BlockSpec((1, GATE_LANES), lambda i: (0, 0))],
        out_specs=[pl.BlockSpec((None, lp, GATE_LANES), lambda i: (i, 0, 0)),
                   pl.BlockSpec((None, 16, lp), lambda i: (i, 0, 0))],
        out_shape=[jax.ShapeDtypeStruct((b, lp, GATE_LANES), F32),
                   jax.ShapeDtypeStruct((b, 16, lp), F32)],
        compiler_params=_params("parallel"),
        name="gates",
    )(zg, bias)


def _mlstm_kernel(has_prev, zm_ref, gcol_ref, grow_ref, gprev_ref, convw_ref, gain_ref, y_ref,
                  prevx_ref, ct_ref, n_ref, m_ref, pc_ref):
    j = pl.program_id(1)
    t = SEQ_BLOCK
    w2 = 2 * M_WIDTH

    @pl.when(j == 0)
    def _():
        prevx_ref[...] = jnp.zeros_like(prevx_ref)
        ct_ref[...] = jnp.zeros_like(ct_ref)
        n_ref[...] = jnp.zeros_like(n_ref)
        m_ref[...] = jnp.zeros_like(m_ref)
        pc_ref[...] = (jnp.broadcast_to(gprev_ref[7:8, :], pc_ref.shape) if has_prev
                       else jnp.zeros_like(pc_ref))

    xqk = zm_ref[:, :w2]
    prev = prevx_ref[...]
    rowi = lax.broadcasted_iota(jnp.int32, (t, w2), 0)
    acc = xqk * convw_ref[CONV_W - 1:CONV_W, :]
    for s in range(1, CONV_W):
        shifted = jnp.where(rowi < s, pltpu.roll(prev, s, 0), pltpu.roll(xqk, s, 0))
        acc = acc + shifted * convw_ref[CONV_W - 1 - s:CONV_W - s, :]
    prevx_ref[...] = xqk
    qk = acc * jax.nn.sigmoid(acc)

    gcol = gcol_ref[...]
    grow = grow_ref[...]
    rr = lax.broadcasted_iota(jnp.int32, (t, t), 0)
    cc = lax.broadcasted_iota(jnp.int32, (t, t), 1)
    causal = rr >= cc
    scale = M_HEAD_DIM ** -0.5

    for h in range(M_HEADS):
        sl = slice(h * M_HEAD_DIM, (h + 1) * M_HEAD_DIM)
        q = qk[:, sl]
        k = qk[:, M_WIDTH + h * M_HEAD_DIM:M_WIDTH + (h + 1) * M_HEAD_DIM] * scale
        v = zm_ref[:, w2 + h * M_HEAD_DIM:w2 + (h + 1) * M_HEAD_DIM]
        o_pre = zm_ref[:, w2 + M_WIDTH + h * M_HEAD_DIM:w2 + M_WIDTH + (h + 1) * M_HEAD_DIM]
        qb, kb, vb = q.astype(BF16), k.astype(BF16), v.astype(BF16)

        li_c = gcol[:, h:h + 1]
        li_r = grow[h:h + 1, :]
        prev_cum = pc_ref[0:1, M_HEADS + h:M_HEADS + h + 1]
        b_c = gcol[:, M_HEADS + h:M_HEADS + h + 1] - prev_cum
        b_r = grow[M_HEADS + h:M_HEADS + h + 1, :] - prev_cum
        m_prev = m_ref[h, 0:1, 0:1]
        ct = ct_ref[h]
        n_row = n_ref[h, 0:1, :]

        dm = jnp.where(causal, (b_c - b_r) + li_r, -jnp.inf)
        m_inter = b_c + m_prev
        m_t = jnp.maximum(m_inter, jnp.max(dm, axis=1, keepdims=True))
        s_mat = lax.dot_general(qb, kb, _NT, preferred_element_type=F32) * jnp.exp(dm - m_t)
        w_inter = jnp.exp(m_inter - m_t)
        num = _dot(s_mat.astype(BF16), vb) + w_inter * _dot(qb, ct.astype(BF16))
        den = jnp.sum(s_mat, axis=1, keepdims=True) + w_inter * jnp.sum(q * n_row, axis=1, keepdims=True)
        hh = num / jnp.maximum(jnp.abs(den), jnp.exp(-m_t))

        b_end = b_c[t - 1:t, :]
        g_c = (b_end - b_c) + li_c
        g_r = (b_end - b_r) + li_r
        m_new = jnp.maximum(b_end + m_prev, jnp.max(g_r, axis=1, keepdims=True))
        decay = jnp.exp(b_end + m_prev - m_new)
        kw = k * jnp.exp(g_c - m_new)
        ct_ref[h] = decay * ct + lax.dot_general(kw.astype(BF16), vb, _TN, preferred_element_type=F32)
        n_ref[h] = jnp.broadcast_to(decay * n_row + jnp.sum(kw, axis=0, keepdims=True), (8, M_HEAD_DIM))
        m_ref[h] = jnp.broadcast_to(m_new, (8, 128))

        hn = hh * lax.rsqrt(jnp.mean(hh * hh, axis=-1, keepdims=True) + EPS) * gain_ref[:, sl]
        y_ref[:, sl] = (hn * jax.nn.sigmoid(o_pre)).astype(y_ref.dtype)

    pc_ref[...] = jnp.broadcast_to(gcol[t - 1:t, :], (8, GATE_LANES))


def _mlstm(zm, gcol, grow, convw, gain, seq):
    b, lp, wz = zm.shape
    t = SEQ_BLOCK
    nb = seq // t + 1
    skip = lp // t - nb
    prev_rows = max(skip * t // 8 - 1, 0)
    return pl.pallas_call(
        functools.partial(_mlstm_kernel, skip > 0),
        grid=(b, nb),
        in_specs=[pl.BlockSpec((None, t, wz), lambda i, j: (i, j + skip, 0)),
                  pl.BlockSpec((None, t, GATE_LANES), lambda i, j: (i, j + skip, 0)),
                  pl.BlockSpec((None, 16, t), lambda i, j: (i, 0, j + skip)),
                  pl.BlockSpec((None, 8, GATE_LANES), lambda i, j: (i, prev_rows, 0)),
                  pl.BlockSpec(convw.shape, lambda i, j: (0, 0)),
                  pl.BlockSpec(gain.shape, lambda i, j: (0, 0))],
        out_specs=pl.BlockSpec((None, t, M_WIDTH), lambda i, j: (i, jnp.maximum(j - 1, 0), 0)),
        out_shape=jax.ShapeDtypeStruct((b, seq, M_WIDTH), BF16),
        scratch_shapes=[pltpu.VMEM((t, 2 * M_WIDTH), F32),
                        pltpu.VMEM((M_HEADS, M_HEAD_DIM, M_HEAD_DIM), F32),
                        pltpu.VMEM((M_HEADS, 8, M_HEAD_DIM), F32),
                        pltpu.VMEM((M_HEADS, 8, 128), F32),
                        pltpu.VMEM((8, GATE_LANES), F32)],
        compiler_params=_params("parallel", "arbitrary"),
        name="mlstm",
    )(zm, gcol, grow, gcol, convw, gain)


def _headnorm(x, g):
    return x * lax.rsqrt(jnp.mean(x * x, axis=-1, keepdims=True) + EPS) * g


_FOX_KDIM = 128


def _split3(c):
    c0 = c.astype(BF16).astype(F32)
    r1 = c - c0
    c1 = r1.astype(BF16).astype(F32)
    c2 = (r1 - c1).astype(BF16).astype(F32)
    return c0, c1, c2


def _fox_kernel(first_real, q_ref, k_ref, v_ref, gcol_ref, gq_ref, gk_ref, y_ref,
                kx_ref, vt_ref, qx_ref, *state_refs):
    qi = pl.program_id(1) + 1
    t = FOX_BLOCK
    d = F_HEAD_DIM
    lp = k_ref.shape[0]
    elane = lax.broadcasted_iota(jnp.int32, (t, _FOX_KDIM - d), 1)

    @pl.when(qi == 1)
    def _():
        def prep(c, carry):
            r0 = pl.multiple_of(c * t, t)
            rows = pl.ds(r0, t)
            vt_ref[:, rows] = v_ref[rows, :].T.astype(BF16)
            g = gcol_ref[rows, :]
            for h in range(F_HEADS):
                kn = _headnorm(k_ref[rows, h * d:(h + 1) * d], gk_ref[...])
                c0, c1, c2 = _split3(g[:, 2 * M_HEADS + h:2 * M_HEADS + h + 1])
                extra = jnp.where(elane < 3, 1.0,
                                  jnp.where(elane == 3, -c0, jnp.where(elane == 4, -c1,
                                                                       jnp.where(elane == 5, -c2, 0.0))))
                kx_ref[rows, h * _FOX_KDIM:(h + 1) * _FOX_KDIM] = jnp.concatenate([kn, extra], axis=1).astype(BF16)
            return carry
        lax.fori_loop(0, lp // t, prep, 0)

    scale = d ** -0.5
    gq_rows = gcol_ref[pl.ds(pl.multiple_of(qi * t, t), t), :]
    for h in range(F_HEADS):
        qn = _headnorm(q_ref[:, h * d:(h + 1) * d], gq_ref[...]) * scale
        c0, c1, c2 = _split3(gq_rows[:, 2 * M_HEADS + h:2 * M_HEADS + h + 1])
        extra = jnp.where(elane == 0, c0, jnp.where(elane == 1, c1, jnp.where(elane == 2, c2,
                                                                              jnp.where(elane < 6, 1.0, 0.0))))
        qx_ref[h] = jnp.concatenate([qn, extra], axis=1).astype(BF16)

    acc_refs, m_refs, l_refs = (state_refs[i * F_HEADS:(i + 1) * F_HEADS] for i in range(3))
    for h in range(F_HEADS):
        m_refs[h][...] = jnp.full_like(m_refs[h], NEG_BIG)
        l_refs[h][...] = jnp.zeros_like(l_refs[h])
        acc_refs[h][...] = jnp.zeros_like(acc_refs[h])

    kk = lax.broadcasted_iota(jnp.int32, (t, t), 0)
    qq = lax.broadcasted_iota(jnp.int32, (t, t), 1)

    def step(jb, mask):
        start = pl.multiple_of(jb * t, t)
        scores = []
        for h in range(F_HEADS):
            kx = kx_ref[pl.ds(start, t), h * _FOX_KDIM:(h + 1) * _FOX_KDIM]
            scores.append(lax.dot_general(kx, qx_ref[h], _NT, preferred_element_type=F32))
        probs, alphas = [], []
        for h in range(F_HEADS):
            s = scores[h] if mask is None else jnp.where(mask, scores[h], NEG_BIG)
            m_old = m_refs[h][...]
            m_new = jnp.maximum(m_old, jnp.max(s, axis=0, keepdims=True))
            p = jnp.exp(s - m_new)
            alpha = jnp.exp(m_old - m_new)
            l_refs[h][...] = alpha * l_refs[h][...] + jnp.sum(p, axis=0, keepdims=True)
            m_refs[h][...] = m_new
            probs.append(p.astype(BF16))
            alphas.append(alpha)
        for h in range(F_HEADS):
            acc_refs[h][...] = alphas[h] * acc_refs[h][...] + _dot(vt_ref[h * d:(h + 1) * d, pl.ds(start, t)],
                                                                  probs[h])

    step(0, kk >= first_real)
    lax.fori_loop(1, qi, lambda jb, c: (step(jb, None), c)[1], 0)
    step(qi, kk <= qq)
    yt = jnp.concatenate([acc_refs[h][...] / l_refs[h][...] for h in range(F_HEADS)], axis=0)
    y_ref[...] = yt.T.astype(y_ref.dtype)


def _fox(zf, gcol, gq, gk, first_real):
    b, lp, _ = zf.shape
    t = FOX_BLOCK
    nb = lp // t
    return pl.pallas_call(
        functools.partial(_fox_kernel, first_real),
        grid=(b, nb - 1),
        in_specs=[pl.BlockSpec((None, t, F_WIDTH), lambda i, j: (i, j + 1, 0)),
                  pl.BlockSpec((None, lp, F_WIDTH), lambda i, j: (i, 0, 1)),
                  pl.BlockSpec((None, lp, F_WIDTH), lambda i, j: (i, 0, 2)),
                  pl.BlockSpec((None, lp, GATE_LANES), lambda i, j: (i, 0, 0)),
                  pl.BlockSpec(gq.shape, lambda i, j: (0, 0)),
                  pl.BlockSpec(gk.shape, lambda i, j: (0, 0))],
        out_specs=pl.BlockSpec((None, t, F_WIDTH), lambda i, j: (i, j, 0)),
        out_shape=jax.ShapeDtypeStruct((b, lp - t, F_WIDTH), BF16),
        scratch_shapes=[pltpu.VMEM((lp, F_HEADS * _FOX_KDIM), BF16),
                        pltpu.VMEM((F_WIDTH, lp), BF16),
                        pltpu.VMEM((F_HEADS, t, _FOX_KDIM), BF16)]
                       + [pltpu.VMEM((F_HEAD_DIM, t), F32)] * F_HEADS
                       + [pltpu.VMEM((1, t), F32)] * F_HEADS
                       + [pltpu.VMEM((1, t), F32)] * F_HEADS,
        compiler_params=_params("parallel", "arbitrary"),
        name="fox",
    )(zf, zf, zf, gcol, gq, gk)


def _outproj_kernel(ym_ref, yf_ref, x_ref, wm_ref, wf_ref, g_ref, h2_ref, xh_ref, xl_ref):
    h2 = x_ref[...] + (_dot(ym_ref[...], wm_ref[...]) + _dot(yf_ref[...], wf_ref[...]))
    h2_ref[...] = h2
    xn = h2 * lax.rsqrt(jnp.mean(h2 * h2, axis=-1, keepdims=True) + EPS) * g_ref[...]
    hi, lo = _split_bf16(xn)
    xh_ref[...] = hi
    xl_ref[...] = lo


def _outproj(ym, yf, x2d, wm, wf, g, tm):
    n, d = x2d.shape
    row = lambda w: pl.BlockSpec((tm, w), lambda i: (i, 0))
    full = lambda a: pl.BlockSpec(a.shape, lambda i: (0, 0))
    return pl.pallas_call(
        _outproj_kernel,
        grid=(n // tm,),
        in_specs=[row(ym.shape[1]), row(yf.shape[1]), row(d), full(wm), full(wf), full(g)],
        out_specs=[row(d), row(d), row(d)],
        out_shape=[jax.ShapeDtypeStruct((n, d), F32),
                   jax.ShapeDtypeStruct((n, d), BF16),
                   jax.ShapeDtypeStruct((n, d), BF16)],
        compiler_params=_params("parallel"),
        name="outproj",
    )(ym, yf, x2d, wm, wf, g)


_TOP = PEER_TOPK + 1
_TOP_ROWS = 24
_RANK_FAR = 127.0


def _top_values(s, want_rank):
    tb = s.shape[1]
    rank = lax.broadcasted_iota(jnp.int32, (_TOP_ROWS, tb), 0).astype(F32)
    vals = jnp.full((_TOP_ROWS, tb), -jnp.inf, F32)
    above = jnp.full(s.shape, _RANK_FAR, F32) if want_rank else None
    cnt = jnp.zeros((1, tb), F32)
    work = s
    for _ in range(_TOP):
        m = jnp.max(work, axis=0, keepdims=True)
        eq = work == m
        k = jnp.sum(jnp.where(eq, 1.0, 0.0), axis=0, keepdims=True)
        vals = jnp.where((rank >= cnt) & (rank < cnt + k), m, vals)
        if want_rank:
            above = jnp.where(eq, cnt, above)
        cnt = cnt + k
        work = jnp.where(eq, -jnp.inf, work)
    return vals, above


def _pair_threshold(a, b):
    k = PEER_TOPK
    slabs = [a[0:1, :] + b[0:_TOP_ROWS, :]]
    for r in range(1, 8):
        slabs.append(a[r:r + 1, :] + b[0:8, :])
    slabs.append(a[8:_TOP_ROWS, :] + b[0:1, :])
    work = jnp.concatenate(slabs, axis=0)
    top = a[0:1, :] + b[0:1, :]
    cnt = jnp.zeros_like(top)
    v_k = top
    v_k1 = top
    z = jnp.zeros_like(top)
    for _ in range(_TOP):
        m = jnp.max(work, axis=0, keepdims=True)
        eq = work == m
        n_eq = jnp.sum(jnp.where(eq, 1.0, 0.0), axis=0, keepdims=True)
        take = jnp.clip(k - cnt, 0.0, n_eq)
        z = z + take * jnp.exp(m - top)
        v_k = jnp.where(cnt < k, m, v_k)
        v_k1 = jnp.where(cnt < k + 1, m, v_k1)
        cnt = cnt + n_eq
        work = jnp.where(eq, -jnp.inf, work)
    return 0.5 * (v_k + v_k1), top, z


def _peer_route_kernel(xh_ref, xl_ref, wqh_ref, wql_ref, kh_ref, kl_ref, r2_ref, e2_ref, n_ref, e1_ref):
    q = _dot3(xh_ref[...], xl_ref[...], wqh_ref[...], wql_ref[...])
    for h in range(PEER_HEADS):
        st = []
        for p in range(2):
            i = 2 * h + p
            qh, ql = _split_bf16(q[:, i * PEER_HALF:(i + 1) * PEER_HALF])
            st.append(_dot3(kh_ref[i], kl_ref[i], qh, ql, _NT))
        a, _ = _top_values(st[0], False)
        b, above2 = _top_values(st[1], True)
        thr, top, z = _pair_threshold(a, b)
        need = thr - st[0]
        n1 = jnp.zeros_like(need)
        for c in range(_TOP):
            n1 = n1 + jnp.where(b[c:c + 1, :] >= need, 1.0, 0.0)
        r2_ref[h] = pltpu.bitcast(above2.astype(BF16), jnp.uint32)
        e2_ref[h] = pltpu.bitcast(jnp.exp(st[1] - b[0:1, :]).astype(BF16), jnp.uint32)
        n_ref[h] = n1
        e1_ref[h] = jnp.exp(st[0] - a[0:1, :]) * (1.0 / z)


def _peer_route(xh, xl, wqh, wql, kh, kl, tb):
    n, d = xh.shape
    row = pl.BlockSpec((tb, d), lambda i: (i, 0))
    full2 = lambda a: pl.BlockSpec(a.shape, lambda i: (0, 0))
    full3 = lambda a: pl.BlockSpec(a.shape, lambda i: (0, 0, 0))
    out = lambda rows: pl.BlockSpec((PEER_HEADS, rows, tb), lambda i: (0, 0, i))
    shape = lambda rows, dt: jax.ShapeDtypeStruct((PEER_HEADS, rows, n), dt)
    return pl.pallas_call(
        _peer_route_kernel,
        grid=(n // tb,),
        in_specs=[row, row, full2(wqh), full2(wql), full3(kh), full3(kl)],
        out_specs=[out(N_KEYS // 2), out(N_KEYS // 2), out(N_KEYS), out(N_KEYS)],
        out_shape=[shape(N_KEYS // 2, jnp.uint32), shape(N_KEYS // 2, jnp.uint32),
                   shape(N_KEYS, F32), shape(N_KEYS, F32)],
        compiler_params=_params("parallel"),
        name="peer_route",
    )(xh, xl, wqh, wql, kh, kl)


def _gelu(x):
    return 0.5 * x * (1.0 + lax.erf(x * (2.0 ** -0.5)))


_PACK = 16


_ROWS_PER_PIECE = 2
_PIECES_PER_CHUNK = 2


def _peer_expert_kernel(ib, xn_ref, u_ref, vt_ref, r2_ref, e2_ref, n_ref, e1_ref, h2_ref, o_ref,
                        acc_ref, w_ref, act_ref):
    e = pl.program_id(1)
    tb = xn_ref.shape[0]
    d = vt_ref.shape[0]
    piece = _ROWS_PER_PIECE * N_KEYS
    chunk = _PIECES_PER_CHUNK * piece
    n_chunks = ib * N_KEYS // chunk
    out_rows = d // _PIECES_PER_CHUNK

    @pl.when(e == 0)
    def _():
        acc_ref[...] = jnp.zeros_like(acc_ref)

    def act_piece(c, k):
        rows = slice(c * chunk + k * piece, c * chunk + (k + 1) * piece)
        act_ref[c % 2, k * piece:(k + 1) * piece, :] = lax.dot_general(
            u_ref[rows, :], xn_ref[...], _NT, preferred_element_type=F32)

    def out_piece(c, k):
        rows = slice(k * out_rows, (k + 1) * out_rows)
        acc_ref[rows, :] += _dot(vt_ref[rows, c * chunk:(c + 1) * chunk], w_ref[c * chunk:(c + 1) * chunk, :])

    def gate_piece(c, k):
        i0 = (c * chunk + k * piece) // N_KEYS
        for tc in range(tb // 128):
            cols = slice(tc * 128, (tc + 1) * 128)
            bcast = lambda ref, h, i: jnp.broadcast_to(ref[h, i:i + 1, cols], (_PACK, 128)).astype(BF16)
            gates = [[None] * (N_KEYS // _PACK) for _ in range(_ROWS_PER_PIECE)]
            for h in range(PEER_HEADS):
                n1 = [bcast(n_ref, h, i0 + r) for r in range(_ROWS_PER_PIECE)]
                e1 = [bcast(e1_ref, h, i0 + r) for r in range(_ROWS_PER_PIECE)]
                for g in range(N_KEYS // _PACK):
                    ks = slice(g * _PACK // 2, (g + 1) * _PACK // 2)
                    above2 = pltpu.bitcast(r2_ref[h, ks, cols], BF16)
                    e2 = pltpu.bitcast(e2_ref[h, ks, cols], BF16)
                    for r in range(_ROWS_PER_PIECE):
                        term = e1[r] * jnp.where(above2 < n1[r], e2, jnp.zeros_like(e2))
                        gates[r][g] = term if gates[r][g] is None else gates[r][g] + term
            for r in range(_ROWS_PER_PIECE):
                for g in range(N_KEYS // _PACK):
                    lo = k * piece + r * N_KEYS + g * _PACK
                    act = act_ref[c % 2, lo:lo + _PACK, cols]
                    w_ref[c * chunk + lo:c * chunk + lo + _PACK, cols] = gates[r][g] * _gelu(act).astype(BF16)

    for k in range(_PIECES_PER_CHUNK):
        act_piece(0, k)
    for c in range(n_chunks):
        for k in range(_PIECES_PER_CHUNK):
            if c + 1 < n_chunks:
                act_piece(c + 1, k)
            if c > 0:
                out_piece(c - 1, k)
            gate_piece(c, k)
    for k in range(_PIECES_PER_CHUNK):
        out_piece(n_chunks - 1, k)

    @pl.when(e == pl.num_programs(1) - 1)
    def _():
        o_ref[...] = h2_ref[...] + acc_ref[...].T


def _peer_experts(xn, u, vt, r2, e2, n1, e1, h2, tb, ib):
    n, d = xn.shape
    ne = u.shape[0]
    eb = ib * N_KEYS
    chunk = _PIECES_PER_CHUNK * _ROWS_PER_PIECE * N_KEYS
    assert eb % chunk == 0
    gate_full = pl.BlockSpec((PEER_HEADS, N_KEYS // 2, tb), lambda t, e: (0, 0, t))
    gate_rows = pl.BlockSpec((PEER_HEADS, ib, tb), lambda t, e: (0, e, t))
    return pl.pallas_call(
        functools.partial(_peer_expert_kernel, ib),
        grid=(n // tb, ne // eb),
        in_specs=[pl.BlockSpec((tb, d), lambda t, e: (t, 0)),
                  pl.BlockSpec((eb, d), lambda t, e: (e, 0)),
                  pl.BlockSpec((d, eb), lambda t, e: (0, e)),
                  gate_full, gate_full, gate_rows, gate_rows,
                  pl.BlockSpec((tb, d), lambda t, e: (t, 0))],
        out_specs=pl.BlockSpec((tb, d), lambda t, e: (t, 0)),
        out_shape=jax.ShapeDtypeStruct((n, d), F32),
        scratch_shapes=[pltpu.VMEM((d, tb), F32),
                        pltpu.VMEM((eb, tb), BF16),
                        pltpu.VMEM((2, chunk, tb), F32)],
        compiler_params=_params("parallel", "arbitrary"),
        name="peer_experts",
    )(xn, u, vt, r2, e2, n1, e1, h2)


def _tile(n, prefer):
    for t in prefer:
        if n % t == 0:
            return t
    raise ValueError(f"no tile for {n}")


def kernel(x, meta_tokens, norm_mix, w_in, conv_qk, b_igate, b_fgate_m, m_out_norm, b_fgate_f,
           f_q_norm, f_k_norm, w_out, norm_ffn, peer_query, peer_sub_keys, peer_u, peer_v):
    assert w_in.shape[0] == 1, "single-layer block"
    b, seq, d = x.shape
    t = FOX_BLOCK
    assert seq % t == 0 and N_META <= SEQ_BLOCK and t % SEQ_BLOCK == 0
    first_real = t - N_META
    lp = seq + t

    meta = jnp.broadcast_to(meta_tokens.astype(x.dtype)[None], (b, N_META, d))
    hpad = jnp.concatenate([jnp.zeros((b, first_real, d), x.dtype), meta, x], axis=1).reshape(b * lp, d)
    w = w_in[0]
    o = 0
    cols = {}
    for name, size in (("mq", M_WIDTH), ("mk", M_WIDTH), ("mv", M_WIDTH), ("mo", M_WIDTH), ("mi", M_HEADS),
                       ("mf", M_HEADS), ("fq", F_WIDTH), ("fk", F_WIDTH), ("fv", F_WIDTH), ("ff", F_HEADS)):
        cols[name] = w[:, o:o + size]
        o += size
    w_m = jnp.concatenate([cols["mq"], cols["mk"], cols["mv"], cols["mo"]], axis=1).astype(BF16)
    w_f = jnp.concatenate([cols["fq"], cols["fk"], cols["fv"]], axis=1).astype(BF16)
    n_gate = 2 * M_HEADS + F_HEADS
    w_g = jnp.concatenate([cols["mi"], cols["mf"], cols["ff"], jnp.zeros((d, GATE_LANES - n_gate), F32)], axis=1)
    w_gh, w_gl = _split_bf16(w_g)
    gate_bias = jnp.concatenate([b_igate[0], b_fgate_m[0], b_fgate_f[0],
                                 jnp.zeros((GATE_LANES - n_gate,), F32)]).reshape(1, GATE_LANES)

    zm, zf, zg = _inproj(hpad, norm_mix[0].reshape(1, d), w_m, w_f, w_gh, w_gl, _tile(b * lp, (512, 256, 128)))
    gcol, grow = _gates(zg.reshape(b, lp, GATE_LANES), gate_bias)
    y_m = _mlstm(zm.reshape(b, lp, 4 * M_WIDTH), gcol, grow, conv_qk[0], m_out_norm[0].reshape(1, M_WIDTH), seq)
    y_f = _fox(zf.reshape(b, lp, 3 * F_WIDTH), gcol, f_q_norm[0].reshape(1, F_HEAD_DIM),
               f_k_norm[0].reshape(1, F_HEAD_DIM), first_real)
    n = b * seq
    wo = w_out[0].astype(BF16)
    h2, xh, xl = _outproj(y_m.reshape(n, M_WIDTH), y_f.reshape(n, F_WIDTH), x.reshape(n, d),
                          wo[:M_WIDTH], wo[M_WIDTH:], norm_ffn[0].reshape(1, d), _tile(n, (512, 256, 128)))

    wqh, wql = _split_bf16(peer_query[0])
    kh, kl = _split_bf16(peer_sub_keys[0].reshape(2 * PEER_HEADS, N_KEYS, PEER_HALF))
    r2, e2, n1, e1 = _peer_route(xh, xl, wqh, wql, kh, kl, _tile(n, (256, 128)))
    u = peer_u[0].astype(BF16)
    vt = peer_v[0].astype(BF16).T
    out = _peer_experts(xh, u, vt, r2, e2, n1, e1, h2, _tile(n, (512, 256, 128)), 16)
    return out.reshape(b, seq, d)
```

```python
import functools

import jax
import jax.numpy as jnp
from jax import lax
from jax.experimental import pallas as pl
from jax.experimental.pallas import tpu as pltpu

F32 = jnp.float32
BF16 = jnp.bfloat16

EPS = 1e-6
N_META = 16
M_HEADS = 4
M_HEAD_DIM = 128
M_WIDTH = M_HEADS * M_HEAD_DIM
CONV_W = 4
F_HEADS = 8
F_HEAD_DIM = 64
F_WIDTH = F_HEADS * F_HEAD_DIM
PEER_HEADS = 8
N_KEYS = 128
PEER_TOPK = 16
PEER_HALF = 128

SEQ_BLOCK = 128
FOX_BLOCK = 256
GATE_LANES = 128
NEG_BIG = -1e30
VMEM_LIMIT = 56 * 1024 * 1024

_NT = (((1,), (1,)), ((), ()))
_TN = (((0,), (0,)), ((), ()))


def _params(*sem):
    return pltpu.CompilerParams(dimension_semantics=sem, vmem_limit_bytes=VMEM_LIMIT)


def _split_bf16(a):
    hi = a.astype(BF16)
    lo = (a - hi.astype(F32)).astype(BF16)
    return hi, lo


def _dot(a, b):
    return jnp.dot(a, b, preferred_element_type=F32)


def _dot3(ah, al, bh, bl, dims=None):
    if dims is None:
        f = _dot
    else:
        f = lambda x, y: lax.dot_general(x, y, dims, preferred_element_type=F32)
    return f(ah, bh) + (f(ah, bl) + f(al, bh))


def _inproj_kernel(h_ref, g_ref, wm_ref, wf_ref, wgh_ref, wgl_ref, zm_ref, zf_ref, zg_ref):
    h = h_ref[...]
    hn = h * lax.rsqrt(jnp.mean(h * h, axis=-1, keepdims=True) + EPS) * g_ref[...]
    hb, hl = _split_bf16(hn)
    zm_ref[...] = _dot(hb, wm_ref[...])
    zf_ref[...] = _dot(hb, wf_ref[...])
    zg_ref[...] = _dot3(hb, hl, wgh_ref[...], wgl_ref[...])


def _inproj(h2d, g, wm, wf, wgh, wgl, tm):
    n, d = h2d.shape
    full = lambda a: pl.BlockSpec(a.shape, lambda i: (0, 0))
    return pl.pallas_call(
        _inproj_kernel,
        grid=(n // tm,),
        in_specs=[pl.BlockSpec((tm, d), lambda i: (i, 0)), full(g), full(wm), full(wf), full(wgh), full(wgl)],
        out_specs=[pl.BlockSpec((tm, wm.shape[1]), lambda i: (i, 0)),
                   pl.BlockSpec((tm, wf.shape[1]), lambda i: (i, 0)),
                   pl.BlockSpec((tm, GATE_LANES), lambda i: (i, 0))],
        out_shape=[jax.ShapeDtypeStruct((n, wm.shape[1]), F32),
                   jax.ShapeDtypeStruct((n, wf.shape[1]), F32),
                   jax.ShapeDtypeStruct((n, GATE_LANES), F32)],
        compiler_params=_params("parallel"),
        name="inproj",
    )(h2d, g, wm, wf, wgh, wgl)


def _log_sigmoid(x):
    return jnp.minimum(x, 0.0) - jnp.log1p(jnp.exp(-jnp.abs(x)))


def _gates_kernel(zg_ref, bias_ref, gcol_ref, grow_ref):
    lp = zg_ref.shape[0]
    t = SEQ_BLOCK
    r = lax.broadcasted_iota(jnp.int32, (t, t), 0)
    c = lax.broadcasted_iota(jnp.int32, (t, t), 1)
    tri = jnp.where(r >= c, 1.0, 0.0).astype(BF16)
    lane = lax.broadcasted_iota(jnp.int32, (t, GATE_LANES), 1)
    carry = jnp.zeros((1, GATE_LANES), F32)
    for i in range(lp // t):
        pre = zg_ref[i * t:(i + 1) * t, :] + bias_ref[...]
        ls = _log_sigmoid(pre)
        l0 = ls.astype(BF16)
        r1 = ls - l0.astype(F32)
        l1 = r1.astype(BF16)
        l2 = (r1 - l1.astype(F32)).astype(BF16)
        cs = (_dot(tri, l0) + (_dot(tri, l1) + _dot(tri, l2))) + carry
        carry = cs[t - 1:t, :]
        out = jnp.where(lane < M_HEADS, pre, cs)
        gcol_ref[i * t:(i + 1) * t, :] = out
        grow_ref[:, i * t:(i + 1) * t] = out.T[:16, :]


def _gates(zg, bias):
    b, lp, _ = zg.shape
    return pl.pallas_call(
        _gates_kernel,
        grid=(b,),
        in_specs=[pl.BlockSpec((None, lp, GATE_LANES), lambda i: (i, 0, 0)),
                  pl.BlockSpec((1, GATE_LANES), lambda i: (0, 0))],
        out_specs=[pl.BlockSpec((None, lp, GATE_LANES), lambda i: (i, 0, 0)),
                   pl.BlockSpec((None, 16, lp), lambda i: (i, 0, 0))],
        out_shape=[jax.ShapeDtypeStruct((b, lp, GATE_LANES), F32),
                   jax.ShapeDtypeStruct((b, 16, lp), F32)],
        compiler_params=_params("parallel"),
        name="gates",
    )(zg, bias)


def _mlstm_kernel(has_prev, zm_ref, gcol_ref, grow_ref, gprev_ref, convw_ref, gain_ref, y_ref,
                  prevx_ref, ct_ref, n_ref, m_ref, pc_ref):
    j = pl.program_id(1)
    t = SEQ_BLOCK
    w2 = 2 * M_WIDTH

    @pl.when(j == 0)
    def _():
        prevx_ref[...] = jnp.zeros_like(prevx_ref)
        ct_ref[...] = jnp.zeros_like(ct_ref)
        n_ref[...] = jnp.zeros_like(n_ref)
        m_ref[...] = jnp.zeros_like(m_ref)
        pc_ref[...] = (jnp.broadcast_to(gprev_ref[7:8, :], pc_ref.shape) if has_prev
                       else jnp.zeros_like(pc_ref))

    xqk = zm_ref[:, :w2]
    prev = prevx_ref[...]
    rowi = lax.broadcasted_iota(jnp.int32, (t, w2), 0)
    acc = xqk * convw_ref[CONV_W - 1:CONV_W, :]
    for s in range(1, CONV_W):
        shifted = jnp.where(rowi < s, pltpu.roll(prev, s, 0), pltpu.roll(xqk, s, 0))
        acc = acc + shifted * convw_ref[CONV_W - 1 - s:CONV_W - s, :]
    prevx_ref[...] = xqk
    qk = acc * jax.nn.sigmoid(acc)

    gcol = gcol_ref[...]
    grow = grow_ref[...]
    rr = lax.broadcasted_iota(jnp.int32, (t, t), 0)
    cc = lax.broadcasted_iota(jnp.int32, (t, t), 1)
    causal = rr >= cc
    scale = M_HEAD_DIM ** -0.5

    for h in range(M_HEADS):
        sl = slice(h * M_HEAD_DIM, (h + 1) * M_HEAD_DIM)
        q = qk[:, sl]
        k = qk[:, M_WIDTH + h * M_HEAD_DIM:M_WIDTH + (h + 1) * M_HEAD_DIM] * scale
        v = zm_ref[:, w2 + h * M_HEAD_DIM:w2 + (h + 1) * M_HEAD_DIM]
        o_pre = zm_ref[:, w2 + M_WIDTH + h * M_HEAD_DIM:w2 + M_WIDTH + (h + 1) * M_HEAD_DIM]
        qb, kb, vb = q.astype(BF16), k.astype(BF16), v.astype(BF16)

        li_c = gcol[:, h:h + 1]
        li_r = grow[h:h + 1, :]
        prev_cum = pc_ref[0:1, M_HEADS + h:M_HEADS + h + 1]
        b_c = gcol[:, M_HEADS + h:M_HEADS + h + 1] - prev_cum
        b_r = grow[M_HEADS + h:M_HEADS + h + 1, :] - prev_cum
        m_prev = m_ref[h, 0:1, 0:1]
        ct = ct_ref[h]
        n_row = n_ref[h, 0:1, :]

        dm = jnp.where(causal, (b_c - b_r) + li_r, -jnp.inf)
        m_inter = b_c + m_prev
        m_t = jnp.maximum(m_inter, jnp.max(dm, axis=1, keepdims=True))
        s_mat = lax.dot_general(qb, kb, _NT, preferred_element_type=F32) * jnp.exp(dm - m_t)
        w_inter = jnp.exp(m_inter - m_t)
        num = _dot(s_mat.astype(BF16), vb) + w_inter * _dot(qb, ct.astype(BF16))
        den = jnp.sum(s_mat, axis=1, keepdims=True) + w_inter * jnp.sum(q * n_row, axis=1, keepdims=True)
        hh = num / jnp.maximum(jnp.abs(den), jnp.exp(-m_t))

        b_end = b_c[t - 1:t, :]
        g_c = (b_end - b_c) + li_c
        g_r = (b_end - b_r) + li_r
        m_new = jnp.maximum(b_end + m_prev, jnp.max(g_r, axis=1, keepdims=True))
        decay = jnp.exp(b_end + m_prev - m_new)
        kw = k * jnp.exp(g_c - m_new)
        ct_ref[h] = decay * ct + lax.dot_general(kw.astype(BF16), vb, _TN, preferred_element_type=F32)
        n_ref[h] = jnp.broadcast_to(decay * n_row + jnp.sum(kw, axis=0, keepdims=True), (8, M_HEAD_DIM))
        m_ref[h] = jnp.broadcast_to(m_new, (8, 128))

        hn = hh * lax.rsqrt(jnp.mean(hh * hh, axis=-1, keepdims=True) + EPS) * gain_ref[:, sl]
        y_ref[:, sl] = (hn * jax.nn.sigmoid(o_pre)).astype(y_ref.dtype)

    pc_ref[...] = jnp.broadcast_to(gcol[t - 1:t, :], (8, GATE_LANES))


def _mlstm(zm, gcol, grow, convw, gain, seq):
    b, lp, wz = zm.shape
    t = SEQ_BLOCK
    nb = seq // t + 1
    skip = lp // t - nb
    prev_rows = max(skip * t // 8 - 1, 0)
    return pl.pallas_call(
        functools.partial(_mlstm_kernel, skip > 0),
        grid=(b, nb),
        in_specs=[pl.BlockSpec((None, t, wz), lambda i, j: (i, j + skip, 0)),
                  pl.BlockSpec((None, t, GATE_LANES), lambda i, j: (i, j + skip, 0)),
                  pl.BlockSpec((None, 16, t), lambda i, j: (i, 0, j + skip)),
                  pl.BlockSpec((None, 8, GATE_LANES), lambda i, j: (i, prev_rows, 0)),
                  pl.BlockSpec(convw.shape, lambda i, j: (0, 0)),
                  pl.BlockSpec(gain.shape, lambda i, j: (0, 0))],
        out_specs=pl.BlockSpec((None, t, M_WIDTH), lambda i, j: (i, jnp.maximum(j - 1, 0), 0)),
        out_shape=jax.ShapeDtypeStruct((b, seq, M_WIDTH), BF16),
        scratch_shapes=[pltpu.VMEM((t, 2 * M_WIDTH), F32),
                        pltpu.VMEM((M_HEADS, M_HEAD_DIM, M_HEAD_DIM), F32),
                        pltpu.VMEM((M_HEADS, 8, M_HEAD_DIM), F32),
                        pltpu.VMEM((M_HEADS, 8, 128), F32),
                        pltpu.VMEM((8, GATE_LANES), F32)],
        compiler_params=_params("parallel", "arbitrary"),
        name="mlstm",
    )(zm, gcol, grow, gcol, convw, gain)


def _headnorm(x, g):
    return x * lax.rsqrt(jnp.mean(x * x, axis=-1, keepdims=True) + EPS) * g


_FOX_KDIM = 128


def _split3(c):
    c0 = c.astype(BF16).astype(F32)
    r1 = c - c0
    c1 = r1.astype(BF16).astype(F32)
    c2 = (r1 - c1).astype(BF16).astype(F32)
    return c0, c1, c2


def _fox_kernel(first_real, q_ref, k_ref, v_ref, gcol_ref, gq_ref, gk_ref, y_ref,
                kx_ref, vt_ref, qx_ref, *state_refs):
    qi = pl.program_id(1) + 1
    t = FOX_BLOCK
    d = F_HEAD_DIM
    lp = k_ref.shape[0]
    elane = lax.broadcasted_iota(jnp.int32, (t, _FOX_KDIM - d), 1)

    @pl.when(qi == 1)
    def _():
        def prep(c, carry):
            r0 = pl.multiple_of(c * t, t)
            rows = pl.ds(r0, t)
            vt_ref[:, rows] = v_ref[rows, :].T.astype(BF16)
            g = gcol_ref[rows, :]
            for h in range(F_HEADS):
                kn = _headnorm(k_ref[rows, h * d:(h + 1) * d], gk_ref[...])
                c0, c1, c2 = _split3(g[:, 2 * M_HEADS + h:2 * M_HEADS + h + 1])
                extra = jnp.where(elane < 3, 1.0,
                                  jnp.where(elane == 3, -c0, jnp.where(elane == 4, -c1,
                                                                       jnp.where(elane == 5, -c2, 0.0))))
                kx_ref[rows, h * _FOX_KDIM:(h + 1) * _FOX_KDIM] = jnp.concatenate([kn, extra], axis=1).astype(BF16)
            return carry
        lax.fori_loop(0, lp // t, prep, 0)

    scale = d ** -0.5
    gq_rows = gcol_ref[pl.ds(pl.multiple_of(qi * t, t), t), :]
    for h in range(F_HEADS):
        qn = _headnorm(q_ref[:, h * d:(h + 1) * d], gq_ref[...]) * scale
        c0, c1, c2 = _split3(gq_rows[:, 2 * M_HEADS + h:2 * M_HEADS + h + 1])
        extra = jnp.where(elane == 0, c0, jnp.where(elane == 1, c1, jnp.where(elane == 2, c2,
                                                                              jnp.where(elane < 6, 1.0, 0.0))))
        qx_ref[h] = jnp.concatenate([qn, extra], axis=1).astype(BF16)

    acc_refs, m_refs, l_refs = (state_refs[i * F_HEADS:(i + 1) * F_HEADS] for i in range(3))
    for h in range(F_HEADS):
        m_refs[h][...] = jnp.full_like(m_refs[h], NEG_BIG)
        l_refs[h][...] = jnp.zeros_like(l_refs[h])
        acc_refs[h][...] = jnp.zeros_like(acc_refs[h])

    kk = lax.broadcasted_iota(jnp.int32, (t, t), 0)
    qq = lax.broadcasted_iota(jnp.int32, (t, t), 1)

    def step(jb, mask):
        start = pl.multiple_of(jb * t, t)
        scores = []
        for h in range(F_HEADS):
            kx = kx_ref[pl.ds(start, t), h * _FOX_KDIM:(h + 1) * _FOX_KDIM]
            scores.append(lax.dot_general(kx, qx_ref[h], _NT, preferred_element_type=F32))
        probs, alphas = [], []
        for h in range(F_HEADS):
            s = scores[h] if mask is None else jnp.where(mask, scores[h], NEG_BIG)
            m_old = m_refs[h][...]
            m_new = jnp.maximum(m_old, jnp.max(s, axis=0, keepdims=True))
            p = jnp.exp(s - m_new)
            alpha = jnp.exp(m_old - m_new)
            l_refs[h][...] = alpha * l_refs[h][...] + jnp.sum(p, axis=0, keepdims=True)
            m_refs[h][...] = m_new
            probs.append(p.astype(BF16))
            alphas.append(alpha)
        for h in range(F_HEADS):
            acc_refs[h][...] = alphas[h] * acc_refs[h][...] + _dot(vt_ref[h * d:(h + 1) * d, pl.ds(start, t)],
                                                                  probs[h])

    step(0, kk >= first_real)
    lax.fori_loop(1, qi, lambda jb, c: (step(jb, None), c)[1], 0)
    step(qi, kk <= qq)
    yt = jnp.concatenate([acc_refs[h][...] / l_refs[h][...] for h in range(F_HEADS)], axis=0)
    y_ref[...] = yt.T.astype(y_ref.dtype)


def _fox(zf, gcol, gq, gk, first_real):
    b, lp, _ = zf.shape
    t = FOX_BLOCK
    nb = lp // t
    return pl.pallas_call(
        functools.partial(_fox_kernel, first_real),
        grid=(b, nb - 1),
        in_specs=[pl.BlockSpec((None, t, F_WIDTH), lambda i, j: (i, j + 1, 0)),
                  pl.BlockSpec((None, lp, F_WIDTH), lambda i, j: (i, 0, 1)),
                  pl.BlockSpec((None, lp, F_WIDTH), lambda i, j: (i, 0, 2)),
                  pl.BlockSpec((None, lp, GATE_LANES), lambda i, j: (i, 0, 0)),
                  pl.BlockSpec(gq.shape, lambda i, j: (0, 0)),
                  pl.BlockSpec(gk.shape, lambda i, j: (0, 0))],
        out_specs=pl.BlockSpec((None, t, F_WIDTH), lambda i, j: (i, j, 0)),
        out_shape=jax.ShapeDtypeStruct((b, lp - t, F_WIDTH), BF16),
        scratch_shapes=[pltpu.VMEM((lp, F_HEADS * _FOX_KDIM), BF16),
                        pltpu.VMEM((F_WIDTH, lp), BF16),
                        pltpu.VMEM((F_HEADS, t, _FOX_KDIM), BF16)]
                       + [pltpu.VMEM((F_HEAD_DIM, t), F32)] * F_HEADS
                       + [pltpu.VMEM((1, t), F32)] * F_HEADS
                       + [pltpu.VMEM((1, t), F32)] * F_HEADS,
        compiler_params=_params("parallel", "arbitrary"),
        name="fox",
    )(zf, zf, zf, gcol, gq, gk)


def _outproj_kernel(ym_ref, yf_ref, x_ref, wm_ref, wf_ref, g_ref, h2_ref, xh_ref, xl_ref):
    h2 = x_ref[...] + (_dot(ym_ref[...], wm_ref[...]) + _dot(yf_ref[...], wf_ref[...]))
    h2_ref[...] = h2
    xn = h2 * lax.rsqrt(jnp.mean(h2 * h2, axis=-1, keepdims=True) + EPS) * g_ref[...]
    hi, lo = _split_bf16(xn)
    xh_ref[...] = hi
    xl_ref[...] = lo


def _outproj(ym, yf, x2d, wm, wf, g, tm):
    n, d = x2d.shape
    row = lambda w: pl.BlockSpec((tm, w), lambda i: (i, 0))
    full = lambda a: pl.BlockSpec(a.shape, lambda i: (0, 0))
    return pl.pallas_call(
        _outproj_kernel,
        grid=(n // tm,),
        in_specs=[row(ym.shape[1]), row(yf.shape[1]), row(d), full(wm), full(wf), full(g)],
        out_specs=[row(d), row(d), row(d)],
        out_shape=[jax.ShapeDtypeStruct((n, d), F32),
                   jax.ShapeDtypeStruct((n, d), BF16),
                   jax.ShapeDtypeStruct((n, d), BF16)],
        compiler_params=_params("parallel"),
        name="outproj",
    )(ym, yf, x2d, wm, wf, g)


_TOP = PEER_TOPK + 1
_TOP_ROWS = 24


def _merge_exchange_network(n):
    t = max(1, (n - 1).bit_length())
    pairs = []
    p = 1 << (t - 1)
    while p > 0:
        q, r, d = 1 << (t - 1), 0, p
        while d > 0:
            pairs += [(i, i + d) for i in range(n - d) if (i & p) == r]
            d, q, r = q - p, q >> 1, p
        p >>= 1
    return pairs


def _pop_largest(s):
    m = s.shape[0] // 8
    cols = [s[8 * i:8 * (i + 1), :] for i in range(m)]
    for i, j in _merge_exchange_network(m):
        cols[i], cols[j] = jnp.maximum(cols[i], cols[j]), jnp.minimum(cols[i], cols[j])
    values, counts = [], []
    for it in range(_TOP):
        best = jnp.max(cols[0], axis=0, keepdims=True)
        eq = cols[0] == best
        values.append(best)
        counts.append(jnp.sum(jnp.where(eq, 1.0, 0.0), axis=0, keepdims=True))
        live = min(m, _TOP - it)
        for d in range(live - 1):
            cols[d] = jnp.where(eq, cols[d + 1], cols[d])
        if live == m:
            cols[m - 1] = jnp.where(eq, -jnp.inf, cols[m - 1])
    return values, counts


def _top_values(s):
    tb = s.shape[1]
    rank = lax.broadcasted_iota(jnp.int32, (_TOP_ROWS, tb), 0).astype(F32)
    vals = None
    cnt = jnp.zeros((1, tb), F32)
    for best, k in zip(*_pop_largest(s)):
        vals = jnp.where(rank >= cnt, best, -jnp.inf if vals is None else vals)
        cnt = cnt + k
    return jnp.where(rank < _TOP, vals, -jnp.inf)


def _pair_threshold(a, b):
    k = PEER_TOPK
    slabs = [a[0:1, :] + b[0:_TOP_ROWS, :]]
    for r in range(1, 8):
        slabs.append(a[r:r + 1, :] + b[0:8, :])
    slabs.append(a[8:_TOP_ROWS, :] + b[0:1, :])
    top = a[0:1, :] + b[0:1, :]
    cnt = jnp.zeros_like(top)
    v_k = top
    v_k1 = top
    z = jnp.zeros_like(top)
    for best, n_eq in zip(*_pop_largest(jnp.concatenate(slabs, axis=0))):
        take = jnp.clip(k - cnt, 0.0, n_eq)
        z = z + take * jnp.exp(best - top)
        v_k = jnp.where(cnt < k, best, v_k)
        v_k1 = jnp.where(cnt < k + 1, best, v_k1)
        cnt = cnt + n_eq
    return 0.5 * (v_k + v_k1), top, z


def _count_leading(sorted_rows, pred):
    out = None
    for c in range(_TOP):
        hit = pred(sorted_rows[c:c + 1, :])
        out = jnp.where(hit, c + 1.0, 0.0 if out is None else out)
    return out


def _peer_route_kernel(xh_ref, xl_ref, wqh_ref, wql_ref, kh_ref, kl_ref, r2_ref, e2_ref, n_ref, e1_ref):
    q = _dot3(xh_ref[...], xl_ref[...], wqh_ref[...], wql_ref[...])
    for h in range(PEER_HEADS):
        st = []
        for p in range(2):
            i = 2 * h + p
            qh, ql = _split_bf16(q[:, i * PEER_HALF:(i + 1) * PEER_HALF])
            st.append(_dot3(kh_ref[i], kl_ref[i], qh, ql, _NT))
        a = _top_values(st[0])
        b = _top_values(st[1])
        thr, top, z = _pair_threshold(a, b)
        need = thr - st[0]
        above2 = _count_leading(b, lambda row: row > st[1])
        n1 = _count_leading(b, lambda row: row >= need)
        r2_ref[h] = pltpu.bitcast(above2.astype(BF16), jnp.uint32)
        e2_ref[h] = pltpu.bitcast(jnp.exp(st[1] - b[0:1, :]).astype(BF16), jnp.uint32)
        n_ref[h] = n1
        e1_ref[h] = jnp.exp(st[0] - a[0:1, :]) * (0.5 / z)


def _peer_route(xh, xl, wqh, wql, kh, kl, tb):
    n, d = xh.shape
    row = pl.BlockSpec((tb, d), lambda i: (i, 0))
    full2 = lambda a: pl.BlockSpec(a.shape, lambda i: (0, 0))
    full3 = lambda a: pl.BlockSpec(a.shape, lambda i: (0, 0, 0))
    out = lambda rows: pl.BlockSpec((PEER_HEADS, rows, tb), lambda i: (0, 0, i))
    shape = lambda rows, dt: jax.ShapeDtypeStruct((PEER_HEADS, rows, n), dt)
    return pl.pallas_call(
        _peer_route_kernel,
        grid=(n // tb,),
        in_specs=[row, row, full2(wqh), full2(wql), full3(kh), full3(kl)],
        out_specs=[out(N_KEYS // 2), out(N_KEYS // 2), out(N_KEYS), out(N_KEYS)],
        out_shape=[shape(N_KEYS // 2, jnp.uint32), shape(N_KEYS // 2, jnp.uint32),
                   shape(N_KEYS, F32), shape(N_KEYS, F32)],
        compiler_params=_params("parallel"),
        name="peer_route",
    )(xh, xl, wqh, wql, kh, kl)


def _twice_gelu(x):
    return x * (1.0 + lax.erf(x * (2.0 ** -0.5)))


_PACK = 16


_ROWS_PER_PIECE = 2
_PIECES_PER_CHUNK = 2


def _peer_expert_kernel(ib, xn_ref, u_ref, vt_ref, r2_ref, e2_ref, n_ref, e1_ref, h2_ref, o_ref,
                        acc_ref, w_ref, act_ref):
    e = pl.program_id(1)
    tb = xn_ref.shape[0]
    d = vt_ref.shape[0]
    piece = _ROWS_PER_PIECE * N_KEYS
    chunk = _PIECES_PER_CHUNK * piece
    n_chunks = ib * N_KEYS // chunk
    out_rows = d // _PIECES_PER_CHUNK

    @pl.when(e == 0)
    def _():
        acc_ref[...] = jnp.zeros_like(acc_ref)

    def act_piece(c, k):
        rows = slice(c * chunk + k * piece, c * chunk + (k + 1) * piece)
        act_ref[c % 2, k * piece:(k + 1) * piece, :] = lax.dot_general(
            u_ref[rows, :], xn_ref[...], _NT, preferred_element_type=F32)

    def out_piece(c, k):
        rows = slice(k * out_rows, (k + 1) * out_rows)
        acc_ref[rows, :] += _dot(vt_ref[rows, c * chunk:(c + 1) * chunk], w_ref[c * chunk:(c + 1) * chunk, :])

    def gate_piece(c, k):
        i0 = (c * chunk + k * piece) // N_KEYS
        for tc in range(tb // 128):
            cols = slice(tc * 128, (tc + 1) * 128)
            bcast = lambda ref, h, i: jnp.broadcast_to(ref[h, i:i + 1, cols], (_PACK, 128)).astype(BF16)
            gates = [[None] * (N_KEYS // _PACK) for _ in range(_ROWS_PER_PIECE)]
            for h in range(PEER_HEADS):
                n1 = [bcast(n_ref, h, i0 + r) for r in range(_ROWS_PER_PIECE)]
                e1 = [bcast(e1_ref, h, i0 + r) for r in range(_ROWS_PER_PIECE)]
                for g in range(N_KEYS // _PACK):
                    ks = slice(g * _PACK // 2, (g + 1) * _PACK // 2)
                    above2 = pltpu.bitcast(r2_ref[h, ks, cols], BF16)
                    e2 = pltpu.bitcast(e2_ref[h, ks, cols], BF16)
                    for r in range(_ROWS_PER_PIECE):
                        term = e1[r] * jnp.where(above2 < n1[r], e2, jnp.zeros_like(e2))
                        gates[r][g] = term if gates[r][g] is None else gates[r][g] + term
            for r in range(_ROWS_PER_PIECE):
                for g in range(N_KEYS // _PACK):
                    lo = k * piece + r * N_KEYS + g * _PACK
                    act = act_ref[c % 2, lo:lo + _PACK, cols]
                    w_ref[c * chunk + lo:c * chunk + lo + _PACK, cols] = gates[r][g] * _twice_gelu(act).astype(BF16)

    for k in range(_PIECES_PER_CHUNK):
        act_piece(0, k)
    for c in range(n_chunks):
        for k in range(_PIECES_PER_CHUNK):
            if c + 1 < n_chunks:
                act_piece(c + 1, k)
            if c > 0:
                out_piece(c - 1, k)
            gate_piece(c, k)
    for k in range(_PIECES_PER_CHUNK):
        out_piece(n_chunks - 1, k)

    @pl.when(e == pl.num_programs(1) - 1)
    def _():
        o_ref[...] = h2_ref[...] + acc_ref[...].T


def _peer_experts(xn, u, vt, r2, e2, n1, e1, h2, tb, ib):
    n, d = xn.shape
    ne = u.shape[0]
    eb = ib * N_KEYS
    chunk = _PIECES_PER_CHUNK * _ROWS_PER_PIECE * N_KEYS
    assert eb % chunk == 0
    gate_full = pl.BlockSpec((PEER_HEADS, N_KEYS // 2, tb), lambda t, e: (0, 0, t))
    gate_rows = pl.BlockSpec((PEER_HEADS, ib, tb), lambda t, e: (0, e, t))
    return pl.pallas_call(
        functools.partial(_peer_expert_kernel, ib),
        grid=(n // tb, ne // eb),
        in_specs=[pl.BlockSpec((tb, d), lambda t, e: (t, 0)),
                  pl.BlockSpec((eb, d), lambda t, e: (e, 0)),
                  pl.BlockSpec((d, eb), lambda t, e: (0, e)),
                  gate_full, gate_full, gate_rows, gate_rows,
                  pl.BlockSpec((tb, d), lambda t, e: (t, 0))],
        out_specs=pl.BlockSpec((tb, d), lambda t, e: (t, 0)),
        out_shape=jax.ShapeDtypeStruct((n, d), F32),
        scratch_shapes=[pltpu.VMEM((d, tb), F32),
                        pltpu.VMEM((eb, tb), BF16),
                        pltpu.VMEM((2, chunk, tb), F32)],
        compiler_params=_params("parallel", "arbitrary"),
        name="peer_experts",
    )(xn, u, vt, r2, e2, n1, e1, h2)


def _tile(n, prefer):
    for t in prefer:
        if n % t == 0:
            return t
    raise ValueError(f"no tile for {n}")


def kernel(x, meta_tokens, norm_mix, w_in, conv_qk, b_igate, b_fgate_m, m_out_norm, b_fgate_f,
           f_q_norm, f_k_norm, w_out, norm_ffn, peer_query, peer_sub_keys, peer_u, peer_v):
    assert w_in.shape[0] == 1, "single-layer block"
    b, seq, d = x.shape
    t = FOX_BLOCK
    assert seq % t == 0 and N_META <= SEQ_BLOCK and t % SEQ_BLOCK == 0
    first_real = t - N_META
    lp = seq + t

    meta = jnp.broadcast_to(meta_tokens.astype(x.dtype)[None], (b, N_META, d))
    hpad = jnp.concatenate([jnp.zeros((b, first_real, d), x.dtype), meta, x], axis=1).reshape(b * lp, d)
    w = w_in[0]
    o = 0
    cols = {}
    for name, size in (("mq", M_WIDTH), ("mk", M_WIDTH), ("mv", M_WIDTH), ("mo", M_WIDTH), ("mi", M_HEADS),
                       ("mf", M_HEADS), ("fq", F_WIDTH), ("fk", F_WIDTH), ("fv", F_WIDTH), ("ff", F_HEADS)):
        cols[name] = w[:, o:o + size]
        o += size
    w_m = jnp.concatenate([cols["mq"], cols["mk"], cols["mv"], cols["mo"]], axis=1).astype(BF16)
    w_f = jnp.concatenate([cols["fq"], cols["fk"], cols["fv"]], axis=1).astype(BF16)
    n_gate = 2 * M_HEADS + F_HEADS
    w_g = jnp.concatenate([cols["mi"], cols["mf"], cols["ff"], jnp.zeros((d, GATE_LANES - n_gate), F32)], axis=1)
    w_gh, w_gl = _split_bf16(w_g)
    gate_bias = jnp.concatenate([b_igate[0], b_fgate_m[0], b_fgate_f[0],
                                 jnp.zeros((GATE_LANES - n_gate,), F32)]).reshape(1, GATE_LANES)

    zm, zf, zg = _inproj(hpad, norm_mix[0].reshape(1, d), w_m, w_f, w_gh, w_gl, _tile(b * lp, (512, 256, 128)))
    gcol, grow = _gates(zg.reshape(b, lp, GATE_LANES), gate_bias)
    y_m = _mlstm(zm.reshape(b, lp, 4 * M_WIDTH), gcol, grow, conv_qk[0], m_out_norm[0].reshape(1, M_WIDTH), seq)
    y_f = _fox(zf.reshape(b, lp, 3 * F_WIDTH), gcol, f_q_norm[0].reshape(1, F_HEAD_DIM),
               f_k_norm[0].reshape(1, F_HEAD_DIM), first_real)
    n = b * seq
    wo = w_out[0].astype(BF16)
    h2, xh, xl = _outproj(y_m.reshape(n, M_WIDTH), y_f.reshape(n, F_WIDTH), x.reshape(n, d),
                          wo[:M_WIDTH], wo[M_WIDTH:], norm_ffn[0].reshape(1, d), _tile(n, (512, 256, 128)))

    wqh, wql = _split_bf16(peer_query[0])
    kh, kl = _split_bf16(peer_sub_keys[0].reshape(2 * PEER_HEADS, N_KEYS, PEER_HALF))
    r2, e2, n1, e1 = _peer_route(xh, xl, wqh, wql, kh, kl, _tile(n, (256, 128)))
    u = peer_u[0].astype(BF16)
    vt = peer_v[0].astype(BF16).T
    out = _peer_experts(xh, u, vt, r2, e2, n1, e1, h2, _tile(n, (512, 256, 128)), 16)
    return out.reshape(b, seq, d)
```

```python
import functools

import jax
import jax.numpy as jnp
from jax import lax
from jax.experimental import pallas as pl
from jax.experimental.pallas import tpu as pltpu

F32 = jnp.float32
BF16 = jnp.bfloat16

EPS = 1e-6
N_META = 16
M_HEADS = 4
M_HEAD_DIM = 128
M_WIDTH = M_HEADS * M_HEAD_DIM
CONV_W = 4
F_HEADS = 8
F_HEAD_DIM = 64
F_WIDTH = F_HEADS * F_HEAD_DIM
PEER_HEADS = 8
N_KEYS = 128
PEER_TOPK = 16
PEER_HALF = 128

SEQ_BLOCK = 128
FOX_BLOCK = 256
GATE_LANES = 128
NEG_BIG = -1e30
VMEM_LIMIT = 56 * 1024 * 1024

_NT = (((1,), (1,)), ((), ()))
_TN = (((0,), (0,)), ((), ()))


def _params(*sem, flags=None):
    return pltpu.CompilerParams(dimension_semantics=sem, vmem_limit_bytes=VMEM_LIMIT, flags=flags)


def _split_bf16(a):
    hi = a.astype(BF16)
    lo = (a - hi.astype(F32)).astype(BF16)
    return hi, lo


def _dot(a, b):
    return jnp.dot(a, b, preferred_element_type=F32)


def _dot3(ah, al, bh, bl, dims=None):
    if dims is None:
        f = _dot
    else:
        f = lambda x, y: lax.dot_general(x, y, dims, preferred_element_type=F32)
    return f(ah, bh) + (f(ah, bl) + f(al, bh))


def _inproj_kernel(h_ref, g_ref, wm_ref, wf_ref, wvt_ref, wgh_ref, wgl_ref, zm_ref, zf_ref, vt_ref, zg_ref):
    h = h_ref[...]
    hn = h * lax.rsqrt(jnp.mean(h * h, axis=-1, keepdims=True) + EPS) * g_ref[...]
    hb, hl = _split_bf16(hn)
    zm_ref[...] = _dot(hb, wm_ref[...])
    zf_ref[...] = _dot(hb, wf_ref[...])
    vt_ref[...] = lax.dot_general(wvt_ref[...], hb, _NT, preferred_element_type=F32)
    zg_ref[...] = _dot3(hb, hl, wgh_ref[...], wgl_ref[...])


def _inproj(h2d, g, wm, wf, wvt, wgh, wgl, tm):
    n, d = h2d.shape
    full = lambda a: pl.BlockSpec(a.shape, lambda i: (0, 0))
    return pl.pallas_call(
        _inproj_kernel,
        grid=(n // tm,),
        in_specs=[pl.BlockSpec((tm, d), lambda i: (i, 0)), full(g), full(wm), full(wf), full(wvt), full(wgh),
                  full(wgl)],
        out_specs=[pl.BlockSpec((tm, wm.shape[1]), lambda i: (i, 0)),
                   pl.BlockSpec((tm, wf.shape[1]), lambda i: (i, 0)),
                   pl.BlockSpec((wvt.shape[0], tm), lambda i: (0, i)),
                   pl.BlockSpec((tm, GATE_LANES), lambda i: (i, 0))],
        out_shape=[jax.ShapeDtypeStruct((n, wm.shape[1]), F32),
                   jax.ShapeDtypeStruct((n, wf.shape[1]), F32),
                   jax.ShapeDtypeStruct((wvt.shape[0], n), F32),
                   jax.ShapeDtypeStruct((n, GATE_LANES), F32)],
        compiler_params=_params("parallel"),
        name="inproj",
    )(h2d, g, wm, wf, wvt, wgh, wgl)


def _log_sigmoid(x):
    return jnp.minimum(x, 0.0) - jnp.log1p(jnp.exp(-jnp.abs(x)))


def _gates_kernel(zg_ref, bias_ref, gcol_ref, grow_ref):
    lp = zg_ref.shape[0]
    t = SEQ_BLOCK
    r = lax.broadcasted_iota(jnp.int32, (t, t), 0)
    c = lax.broadcasted_iota(jnp.int32, (t, t), 1)
    tri = jnp.where(r >= c, 1.0, 0.0).astype(BF16)
    lane = lax.broadcasted_iota(jnp.int32, (t, GATE_LANES), 1)
    carry = jnp.zeros((1, GATE_LANES), F32)
    for i in range(lp // t):
        pre = zg_ref[i * t:(i + 1) * t, :] + bias_ref[...]
        ls = _log_sigmoid(pre)
        l0 = ls.astype(BF16)
        r1 = ls - l0.astype(F32)
        l1 = r1.astype(BF16)
        l2 = (r1 - l1.astype(F32)).astype(BF16)
        cs = (_dot(tri, l0) + (_dot(tri, l1) + _dot(tri, l2))) + carry
        carry = cs[t - 1:t, :]
        out = jnp.where(lane < M_HEADS, pre, cs)
        gcol_ref[i * t:(i + 1) * t, :] = out
        grow_ref[:, i * t:(i + 1) * t] = out.T[:16, :]


def _gates(zg, bias):
    b, lp, _ = zg.shape
    return pl.pallas_call(
        _gates_kernel,
        grid=(b,),
        in_specs=[pl.BlockSpec((None, lp, GATE_LANES), lambda i: (i, 0, 0)),
                  pl.BlockSpec((1, GATE_LANES), lambda i: (0, 0))],
        out_specs=[pl.BlockSpec((None, lp, GATE_LANES), lambda i: (i, 0, 0)),
                   pl.BlockSpec((None, 16, lp), lambda i: (i, 0, 0))],
        out_shape=[jax.ShapeDtypeStruct((b, lp, GATE_LANES), F32),
                   jax.ShapeDtypeStruct((b, 16, lp), F32)],
        compiler_params=_params("parallel"),
        name="gates",
    )(zg, bias)


def _mlstm_kernel(has_prev, zm_ref, gcol_ref, grow_ref, gprev_ref, convw_ref, gain_ref, y_ref,
                  prevx_ref, ct_ref, n_ref, m_ref, pc_ref):
    j = pl.program_id(1)
    t = SEQ_BLOCK
    w2 = 2 * M_WIDTH

    @pl.when(j == 0)
    def _():
        prevx_ref[...] = jnp.zeros_like(prevx_ref)
        ct_ref[...] = jnp.zeros_like(ct_ref)
        n_ref[...] = jnp.zeros_like(n_ref)
        m_ref[...] = jnp.zeros_like(m_ref)
        pc_ref[...] = (jnp.broadcast_to(gprev_ref[7:8, :], pc_ref.shape) if has_prev
                       else jnp.zeros_like(pc_ref))

    xqk = zm_ref[:, :w2]
    prev = prevx_ref[...]
    rowi = lax.broadcasted_iota(jnp.int32, (t, w2), 0)
    acc = xqk * convw_ref[CONV_W - 1:CONV_W, :]
    for s in range(1, CONV_W):
        shifted = jnp.where(rowi < s, pltpu.roll(prev, s, 0), pltpu.roll(xqk, s, 0))
        acc = acc + shifted * convw_ref[CONV_W - 1 - s:CONV_W - s, :]
    prevx_ref[...] = xqk
    qk = acc * jax.nn.sigmoid(acc)

    gcol = gcol_ref[...]
    grow = grow_ref[...]
    rr = lax.broadcasted_iota(jnp.int32, (t, t), 0)
    cc = lax.broadcasted_iota(jnp.int32, (t, t), 1)
    causal = rr >= cc
    scale = M_HEAD_DIM ** -0.5

    heads = range(M_HEADS)
    hsl = [slice(h * M_HEAD_DIM, (h + 1) * M_HEAD_DIM) for h in heads]
    q = [qk[:, hsl[h]] for h in heads]
    k = [qk[:, M_WIDTH + h * M_HEAD_DIM:M_WIDTH + (h + 1) * M_HEAD_DIM] * scale for h in heads]
    qb = [x.astype(BF16) for x in q]
    kb = [x.astype(BF16) for x in k]
    vb = [zm_ref[:, w2 + h * M_HEAD_DIM:w2 + (h + 1) * M_HEAD_DIM].astype(BF16) for h in heads]
    ct = [ct_ref[h] for h in heads]
    n_row = [n_ref[h, 0:1, :] for h in heads]
    m_prev = [m_ref[h, 0:1, 0:1] for h in heads]

    qk_dot = [lax.dot_general(qb[h], kb[h], _NT, preferred_element_type=F32) for h in heads]
    q_ct = [_dot(qb[h], ct[h].astype(BF16)) for h in heads]

    li_c = [gcol[:, h:h + 1] for h in heads]
    li_r = [grow[h:h + 1, :] for h in heads]
    prev_cum = [pc_ref[0:1, M_HEADS + h:M_HEADS + h + 1] for h in heads]
    b_c = [gcol[:, M_HEADS + h:M_HEADS + h + 1] - prev_cum[h] for h in heads]
    b_r = [grow[M_HEADS + h:M_HEADS + h + 1, :] - prev_cum[h] for h in heads]
    dm = [jnp.where(causal, (b_c[h] - b_r[h]) + li_r[h], -jnp.inf) for h in heads]
    m_inter = [b_c[h] + m_prev[h] for h in heads]
    m_t = [jnp.maximum(m_inter[h], jnp.max(dm[h], axis=1, keepdims=True)) for h in heads]
    w_inter = [jnp.exp(m_inter[h] - m_t[h]) for h in heads]
    s_mat = [qk_dot[h] * jnp.exp(dm[h] - m_t[h]) for h in heads]

    num = [_dot(s_mat[h].astype(BF16), vb[h]) + w_inter[h] * q_ct[h] for h in heads]
    den = [jnp.sum(s_mat[h], axis=1, keepdims=True)
           + w_inter[h] * jnp.sum(q[h] * n_row[h], axis=1, keepdims=True) for h in heads]
    hh = [num[h] / jnp.maximum(jnp.abs(den[h]), jnp.exp(-m_t[h])) for h in heads]

    b_end = [b_c[h][t - 1:t, :] for h in heads]
    g_c = [(b_end[h] - b_c[h]) + li_c[h] for h in heads]
    g_r = [(b_end[h] - b_r[h]) + li_r[h] for h in heads]
    m_new = [jnp.maximum(b_end[h] + m_prev[h], jnp.max(g_r[h], axis=1, keepdims=True)) for h in heads]
    decay = [jnp.exp(b_end[h] + m_prev[h] - m_new[h]) for h in heads]
    kw = [k[h] * jnp.exp(g_c[h] - m_new[h]) for h in heads]
    for h in heads:
        ct_ref[h] = decay[h] * ct[h] + lax.dot_general(kw[h].astype(BF16), vb[h], _TN, preferred_element_type=F32)
        n_ref[h] = jnp.broadcast_to(decay[h] * n_row[h] + jnp.sum(kw[h], axis=0, keepdims=True), (8, M_HEAD_DIM))
        m_ref[h] = jnp.broadcast_to(m_new[h], (8, 128))

    for h in heads:
        o_pre = zm_ref[:, w2 + M_WIDTH + h * M_HEAD_DIM:w2 + M_WIDTH + (h + 1) * M_HEAD_DIM]
        hn = hh[h] * lax.rsqrt(jnp.mean(hh[h] * hh[h], axis=-1, keepdims=True) + EPS) * gain_ref[:, hsl[h]]
        y_ref[:, hsl[h]] = (hn * jax.nn.sigmoid(o_pre)).astype(y_ref.dtype)

    pc_ref[...] = jnp.broadcast_to(gcol[t - 1:t, :], (8, GATE_LANES))


def _mlstm(zm, gcol, grow, convw, gain, seq):
    b, lp, wz = zm.shape
    t = SEQ_BLOCK
    nb = seq // t + 1
    skip = lp // t - nb
    prev_rows = max(skip * t // 8 - 1, 0)
    return pl.pallas_call(
        functools.partial(_mlstm_kernel, skip > 0),
        grid=(b, nb),
        in_specs=[pl.BlockSpec((None, t, wz), lambda i, j: (i, j + skip, 0)),
                  pl.BlockSpec((None, t, GATE_LANES), lambda i, j: (i, j + skip, 0)),
                  pl.BlockSpec((None, 16, t), lambda i, j: (i, 0, j + skip)),
                  pl.BlockSpec((None, 8, GATE_LANES), lambda i, j: (i, prev_rows, 0)),
                  pl.BlockSpec(convw.shape, lambda i, j: (0, 0)),
                  pl.BlockSpec(gain.shape, lambda i, j: (0, 0))],
        out_specs=pl.BlockSpec((None, t, M_WIDTH), lambda i, j: (i, jnp.maximum(j - 1, 0), 0)),
        out_shape=jax.ShapeDtypeStruct((b, seq, M_WIDTH), BF16),
        scratch_shapes=[pltpu.VMEM((t, 2 * M_WIDTH), F32),
                        pltpu.VMEM((M_HEADS, M_HEAD_DIM, M_HEAD_DIM), F32),
                        pltpu.VMEM((M_HEADS, 8, M_HEAD_DIM), F32),
                        pltpu.VMEM((M_HEADS, 8, 128), F32),
                        pltpu.VMEM((8, GATE_LANES), F32)],
        compiler_params=_params("parallel", "arbitrary"),
        name="mlstm",
    )(zm, gcol, grow, gcol, convw, gain)


def _headnorm(x, g):
    return x * lax.rsqrt(jnp.mean(x * x, axis=-1, keepdims=True) + EPS) * g


_FOX_KDIM = 128
_LOG2E = 1.4426950408889634


def _split3(c):
    c0 = c.astype(BF16).astype(F32)
    r1 = c - c0
    c1 = r1.astype(BF16).astype(F32)
    c2 = (r1 - c1).astype(BF16).astype(F32)
    return c0, c1, c2


def _fox_kernel(first_real, q_ref, k_ref, v_ref, gcol_ref, gq_ref, gk_ref, y_ref,
                kx_ref, vt_ref, qx_ref, *state_refs):
    qi = pl.program_id(1) + 1
    t = FOX_BLOCK
    d = F_HEAD_DIM
    lp = k_ref.shape[0]
    elane = lax.broadcasted_iota(jnp.int32, (t, _FOX_KDIM - d), 1)

    @pl.when(qi == 1)
    def _():
        def prep(c, carry):
            r0 = pl.multiple_of(c * t, t)
            rows = pl.ds(r0, t)
            vt_ref[:, rows] = v_ref[:, rows].astype(BF16)
            g = gcol_ref[rows, :]
            for h in range(F_HEADS):
                kn = _headnorm(k_ref[rows, h * d:(h + 1) * d], gk_ref[...])
                c0, c1, c2 = _split3(g[:, 2 * M_HEADS + h:2 * M_HEADS + h + 1] * _LOG2E)
                extra = jnp.where(elane < 3, 1.0,
                                  jnp.where(elane == 3, -c0, jnp.where(elane == 4, -c1,
                                                                       jnp.where(elane == 5, -c2, 0.0))))
                kx_ref[rows, h * _FOX_KDIM:(h + 1) * _FOX_KDIM] = jnp.concatenate([kn, extra], axis=1).astype(BF16)
            return carry
        lax.fori_loop(0, lp // t, prep, 0)

    scale = d ** -0.5 * _LOG2E
    gq_rows = gcol_ref[pl.ds(pl.multiple_of(qi * t, t), t), :] * _LOG2E
    for h in range(F_HEADS):
        qn = _headnorm(q_ref[:, h * d:(h + 1) * d], gq_ref[...]) * scale
        c0, c1, c2 = _split3(gq_rows[:, 2 * M_HEADS + h:2 * M_HEADS + h + 1])
        extra = jnp.where(elane == 0, c0, jnp.where(elane == 1, c1, jnp.where(elane == 2, c2,
                                                                              jnp.where(elane < 6, 1.0, 0.0))))
        qx_ref[h] = jnp.concatenate([qn, extra], axis=1).astype(BF16)

    acc_refs, m_refs, l_refs = (state_refs[i * F_HEADS:(i + 1) * F_HEADS] for i in range(3))
    for h in range(F_HEADS):
        m_refs[h][...] = jnp.full_like(m_refs[h], NEG_BIG)
        l_refs[h][...] = jnp.zeros_like(l_refs[h])
        acc_refs[h][...] = jnp.zeros_like(acc_refs[h])

    kk = lax.broadcasted_iota(jnp.int32, (t, t), 0)
    qq = lax.broadcasted_iota(jnp.int32, (t, t), 1)

    def step(jb, mask):
        start = pl.multiple_of(jb * t, t)
        scores = []
        for h in range(F_HEADS):
            kx = kx_ref[pl.ds(start, t), h * _FOX_KDIM:(h + 1) * _FOX_KDIM]
            scores.append(lax.dot_general(kx, qx_ref[h], _NT, preferred_element_type=F32))
        probs, alphas = [], []
        for h in range(F_HEADS):
            s = scores[h] if mask is None else jnp.where(mask, scores[h], NEG_BIG)
            m_old = m_refs[h][...]
            m_new = jnp.maximum(m_old, jnp.max(s, axis=0, keepdims=True))
            p = jnp.exp2(s - m_new)
            alpha = jnp.exp2(m_old - m_new)
            l_refs[h][...] = alpha * l_refs[h][...] + jnp.sum(p, axis=0, keepdims=True)
            m_refs[h][...] = m_new
            probs.append(p.astype(BF16))
            alphas.append(alpha)
        for h in range(F_HEADS):
            acc_refs[h][...] = alphas[h] * acc_refs[h][...] + _dot(vt_ref[h * d:(h + 1) * d, pl.ds(start, t)],
                                                                  probs[h])

    step(0, kk >= first_real)
    lax.fori_loop(1, qi, lambda jb, c: (step(jb, None), c)[1], 0)
    step(qi, kk <= qq)
    yt = jnp.concatenate([acc_refs[h][...] / l_refs[h][...] for h in range(F_HEADS)], axis=0)
    y_ref[...] = yt.T.astype(y_ref.dtype)


def _fox(zf, vt, gcol, gq, gk, first_real):
    b, lp, _ = zf.shape
    t = FOX_BLOCK
    nb = lp // t
    return pl.pallas_call(
        functools.partial(_fox_kernel, first_real),
        grid=(b, nb - 1),
        in_specs=[pl.BlockSpec((None, t, F_WIDTH), lambda i, j: (i, j + 1, 0)),
                  pl.BlockSpec((None, lp, F_WIDTH), lambda i, j: (i, 0, 1)),
                  pl.BlockSpec((F_WIDTH, lp), lambda i, j: (0, i)),
                  pl.BlockSpec((None, lp, GATE_LANES), lambda i, j: (i, 0, 0)),
                  pl.BlockSpec(gq.shape, lambda i, j: (0, 0)),
                  pl.BlockSpec(gk.shape, lambda i, j: (0, 0))],
        out_specs=pl.BlockSpec((None, t, F_WIDTH), lambda i, j: (i, j, 0)),
        out_shape=jax.ShapeDtypeStruct((b, lp - t, F_WIDTH), BF16),
        scratch_shapes=[pltpu.VMEM((lp, F_HEADS * _FOX_KDIM), BF16),
                        pltpu.VMEM((F_WIDTH, lp), BF16),
                        pltpu.VMEM((F_HEADS, t, _FOX_KDIM), BF16)]
                       + [pltpu.VMEM((F_HEAD_DIM, t), F32)] * F_HEADS
                       + [pltpu.VMEM((1, t), F32)] * F_HEADS
                       + [pltpu.VMEM((1, t), F32)] * F_HEADS,
        compiler_params=_params("parallel", "arbitrary"),
        name="fox",
    )(zf, zf, vt, gcol, gq, gk)


def _outproj_kernel(ym_ref, yf_ref, x_ref, wm_ref, wf_ref, g_ref, h2_ref, xh_ref, xl_ref):
    h2 = x_ref[...] + (_dot(ym_ref[...], wm_ref[...]) + _dot(yf_ref[...], wf_ref[...]))
    h2_ref[...] = h2
    xn = h2 * lax.rsqrt(jnp.mean(h2 * h2, axis=-1, keepdims=True) + EPS) * g_ref[...]
    hi, lo = _split_bf16(xn)
    xh_ref[...] = hi
    xl_ref[...] = lo


def _outproj(ym, yf, x2d, wm, wf, g, tm):
    n, d = x2d.shape
    row = lambda w: pl.BlockSpec((tm, w), lambda i: (i, 0))
    full = lambda a: pl.BlockSpec(a.shape, lambda i: (0, 0))
    return pl.pallas_call(
        _outproj_kernel,
        grid=(n // tm,),
        in_specs=[row(ym.shape[1]), row(yf.shape[1]), row(d), full(wm), full(wf), full(g)],
        out_specs=[row(d), row(d), row(d)],
        out_shape=[jax.ShapeDtypeStruct((n, d), F32),
                   jax.ShapeDtypeStruct((n, d), BF16),
                   jax.ShapeDtypeStruct((n, d), BF16)],
        compiler_params=_params("parallel"),
        name="outproj",
    )(ym, yf, x2d, wm, wf, g)


_TOP = PEER_TOPK + 1
_TOP_ROWS = 24


def _merge_exchange_network(n):
    t = max(1, (n - 1).bit_length())
    pairs = []
    p = 1 << (t - 1)
    while p > 0:
        q, r, d = 1 << (t - 1), 0, p
        while d > 0:
            pairs += [(i, i + d) for i in range(n - d) if (i & p) == r]
            d, q, r = q - p, q >> 1, p
        p >>= 1
    return pairs


def _pop_largest(s):
    m = s.shape[0] // 8
    cols = [s[8 * i:8 * (i + 1), :] for i in range(m)]
    for i, j in _merge_exchange_network(m):
        cols[i], cols[j] = jnp.maximum(cols[i], cols[j]), jnp.minimum(cols[i], cols[j])
    values, counts = [], []
    for it in range(_TOP):
        best = jnp.max(cols[0], axis=0, keepdims=True)
        eq = cols[0] == best
        values.append(best)
        counts.append(jnp.sum(jnp.where(eq, 1.0, 0.0), axis=0, keepdims=True))
        live = min(m, _TOP - it)
        for d in range(live - 1):
            cols[d] = jnp.where(eq, cols[d + 1], cols[d])
        if live == m:
            cols[m - 1] = jnp.where(eq, -jnp.inf, cols[m - 1])
    return values, counts


def _top_values(s):
    tb = s.shape[1]
    rank = lax.broadcasted_iota(jnp.int32, (_TOP_ROWS, tb), 0).astype(F32)
    vals = None
    cnt = jnp.zeros((1, tb), F32)
    for best, k in zip(*_pop_largest(s)):
        vals = jnp.where(rank >= cnt, best, -jnp.inf if vals is None else vals)
        cnt = cnt + k
    return jnp.where(rank < _TOP, vals, -jnp.inf)


def _pair_threshold(a, b):
    k = PEER_TOPK
    slabs = [a[0:1, :] + b[0:_TOP_ROWS, :]]
    for r in range(1, 8):
        slabs.append(a[r:r + 1, :] + b[0:8, :])
    slabs.append(a[8:_TOP_ROWS, :] + b[0:1, :])
    top = a[0:1, :] + b[0:1, :]
    cnt = jnp.zeros_like(top)
    v_k = top
    v_k1 = top
    z = jnp.zeros_like(top)
    for best, n_eq in zip(*_pop_largest(jnp.concatenate(slabs, axis=0))):
        take = jnp.clip(k - cnt, 0.0, n_eq)
        z = z + take * jnp.exp(best - top)
        v_k = jnp.where(cnt < k, best, v_k)
        v_k1 = jnp.where(cnt < k + 1, best, v_k1)
        cnt = cnt + n_eq
    return 0.5 * (v_k + v_k1), top, z


def _count_leading(sorted_rows, pred):
    out = None
    for c in range(_TOP):
        hit = pred(sorted_rows[c:c + 1, :])
        out = jnp.where(hit, c + 1.0, 0.0 if out is None else out)
    return out


def _peer_route_kernel(xh_ref, xl_ref, wqh_ref, wql_ref, kh_ref, kl_ref, r2_ref, e2_ref, n_ref, e1_ref):
    q = _dot3(xh_ref[...], xl_ref[...], wqh_ref[...], wql_ref[...])
    for h in range(PEER_HEADS):
        st = []
        for p in range(2):
            i = 2 * h + p
            qh, ql = _split_bf16(q[:, i * PEER_HALF:(i + 1) * PEER_HALF])
            st.append(_dot3(kh_ref[i], kl_ref[i], qh, ql, _NT))
        a = _top_values(st[0])
        b = _top_values(st[1])
        thr, top, z = _pair_threshold(a, b)
        need = thr - st[0]
        above2 = _count_leading(b, lambda row: row > st[1])
        n1 = _count_leading(b, lambda row: row >= need)
        r2_ref[h] = pltpu.bitcast(above2.astype(BF16), jnp.uint32)
        e2_ref[h] = pltpu.bitcast(jnp.exp(st[1] - b[0:1, :]).astype(BF16), jnp.uint32)
        n_ref[h] = n1
        e1_ref[h] = jnp.exp(st[0] - a[0:1, :]) * (0.5 / z)


def _peer_route(xh, xl, wqh, wql, kh, kl, tb):
    n, d = xh.shape
    row = pl.BlockSpec((tb, d), lambda i: (i, 0))
    full2 = lambda a: pl.BlockSpec(a.shape, lambda i: (0, 0))
    full3 = lambda a: pl.BlockSpec(a.shape, lambda i: (0, 0, 0))
    out = lambda rows: pl.BlockSpec((PEER_HEADS, rows, tb), lambda i: (0, 0, i))
    shape = lambda rows, dt: jax.ShapeDtypeStruct((PEER_HEADS, rows, n), dt)
    return pl.pallas_call(
        _peer_route_kernel,
        grid=(n // tb,),
        in_specs=[row, row, full2(wqh), full2(wql), full3(kh), full3(kl)],
        out_specs=[out(N_KEYS // 2), out(N_KEYS // 2), out(N_KEYS), out(N_KEYS)],
        out_shape=[shape(N_KEYS // 2, jnp.uint32), shape(N_KEYS // 2, jnp.uint32),
                   shape(N_KEYS, F32), shape(N_KEYS, F32)],
        compiler_params=_params("parallel"),
        name="peer_route",
    )(xh, xl, wqh, wql, kh, kl)


def _twice_gelu(x):
    return x * (1.0 + lax.erf(x * (2.0 ** -0.5)))


_PACK = 16


_ROWS_PER_PIECE = 2
_PIECES_PER_CHUNK = 2


def _pack_rows_kernel(x_ref, o_ref):
    o_ref[...] = pltpu.bitcast(x_ref[...].astype(BF16), jnp.uint32)


def _pack_rows_t_kernel(x_ref, o_ref):
    o_ref[...] = pltpu.bitcast(x_ref[...].T.astype(BF16), jnp.uint32)


def _pack_bf16(x, transpose, rows):
    n, c = x.shape
    if transpose:
        kern, out_shape, out_spec = _pack_rows_t_kernel, (c // 2, n), pl.BlockSpec((c // 2, rows), lambda i: (0, i))
    else:
        kern, out_shape, out_spec = _pack_rows_kernel, (n // 2, c), pl.BlockSpec((rows // 2, c), lambda i: (i, 0))
    return pl.pallas_call(
        kern,
        grid=(n // rows,),
        in_specs=[pl.BlockSpec((rows, c), lambda i: (i, 0))],
        out_specs=out_spec,
        out_shape=jax.ShapeDtypeStruct(out_shape, jnp.uint32),
        compiler_params=_params("parallel"),
        name="pack_t" if transpose else "pack",
    )(x)


def _peer_expert_kernel(ib, xn_ref, u_ref, vt_ref, r2_ref, e2_ref, n_ref, e1_ref, h2_ref, o_ref,
                        acc_ref, w_ref, act_ref):
    e = pl.program_id(1)
    tb = xn_ref.shape[0]
    d = 2 * vt_ref.shape[0]
    piece = _ROWS_PER_PIECE * N_KEYS
    chunk = _PIECES_PER_CHUNK * piece
    n_chunks = ib * N_KEYS // chunk
    out_rows = d // _PIECES_PER_CHUNK

    @pl.when(e == 0)
    def _():
        acc_ref[...] = jnp.zeros_like(acc_ref)

    def act_piece(c, k):
        lo = c * chunk + k * piece
        act_ref[c % 2, k * piece:(k + 1) * piece, :] = lax.dot_general(
            pltpu.bitcast(u_ref[lo // 2:(lo + piece) // 2, :], BF16), xn_ref[...], _NT,
            preferred_element_type=F32)

    def out_piece(c, k):
        rows = slice(k * out_rows, (k + 1) * out_rows)
        vt = pltpu.bitcast(vt_ref[k * out_rows // 2:(k + 1) * out_rows // 2, c * chunk:(c + 1) * chunk], BF16)
        acc_ref[rows, :] += _dot(vt, w_ref[c * chunk:(c + 1) * chunk, :])

    def gate_piece(c, k):
        i0 = (c * chunk + k * piece) // N_KEYS
        for tc in range(tb // 128):
            cols = slice(tc * 128, (tc + 1) * 128)
            bcast = lambda ref, h, i: jnp.broadcast_to(ref[h, i:i + 1, cols], (_PACK, 128)).astype(BF16)
            gates = [[None] * (N_KEYS // _PACK) for _ in range(_ROWS_PER_PIECE)]
            for h in range(PEER_HEADS):
                n1 = [bcast(n_ref, h, i0 + r) for r in range(_ROWS_PER_PIECE)]
                e1 = [bcast(e1_ref, h, i0 + r) for r in range(_ROWS_PER_PIECE)]
                for g in range(N_KEYS // _PACK):
                    ks = slice(g * _PACK // 2, (g + 1) * _PACK // 2)
                    above2 = pltpu.bitcast(r2_ref[h, ks, cols], BF16)
                    e2 = pltpu.bitcast(e2_ref[h, ks, cols], BF16)
                    for r in range(_ROWS_PER_PIECE):
                        term = e1[r] * jnp.where(above2 < n1[r], e2, jnp.zeros_like(e2))
                        gates[r][g] = term if gates[r][g] is None else gates[r][g] + term
            for r in range(_ROWS_PER_PIECE):
                for g in range(N_KEYS // _PACK):
                    lo = k * piece + r * N_KEYS + g * _PACK
                    act = act_ref[c % 2, lo:lo + _PACK, cols]
                    w_ref[c * chunk + lo:c * chunk + lo + _PACK, cols] = gates[r][g] * _twice_gelu(act).astype(BF16)

    for k in range(_PIECES_PER_CHUNK):
        act_piece(0, k)
    for c in range(n_chunks):
        for k in range(_PIECES_PER_CHUNK):
            if c + 1 < n_chunks:
                act_piece(c + 1, k)
            if c > 0:
                out_piece(c - 1, k)
            gate_piece(c, k)
    for k in range(_PIECES_PER_CHUNK):
        out_piece(n_chunks - 1, k)

    @pl.when(e == pl.num_programs(1) - 1)
    def _():
        o_ref[...] = h2_ref[...] + acc_ref[...].T


def _peer_experts(xn, u_packed, vt_packed, r2, e2, n1, e1, h2, tb, ib):
    n, d = xn.shape
    ne = 2 * u_packed.shape[0]
    eb = ib * N_KEYS
    chunk = _PIECES_PER_CHUNK * _ROWS_PER_PIECE * N_KEYS
    assert eb % chunk == 0 and tb % 128 == 0
    gate_full = pl.BlockSpec((PEER_HEADS, N_KEYS // 2, tb), lambda t, e: (0, 0, t))
    gate_rows = pl.BlockSpec((PEER_HEADS, ib, tb), lambda t, e: (0, e, t))
    return pl.pallas_call(
        functools.partial(_peer_expert_kernel, ib),
        grid=(n // tb, ne // eb),
        in_specs=[pl.BlockSpec((tb, d), lambda t, e: (t, 0)),
                  pl.BlockSpec((eb // 2, d), lambda t, e: (e, 0)),
                  pl.BlockSpec((d // 2, eb), lambda t, e: (0, e)),
                  gate_full, gate_full, gate_rows, gate_rows,
                  pl.BlockSpec((tb, d), lambda t, e: (t, 0))],
        out_specs=pl.BlockSpec((tb, d), lambda t, e: (t, 0)),
        out_shape=jax.ShapeDtypeStruct((n, d), F32),
        scratch_shapes=[pltpu.VMEM((d, tb), F32),
                        pltpu.VMEM((eb, tb), BF16),
                        pltpu.VMEM((2, chunk, tb), F32)],
        compiler_params=_params("parallel", "arbitrary"),
        name="peer_experts",
    )(xn, u_packed, vt_packed, r2, e2, n1, e1, h2)


def _tile(n, prefer):
    for t in prefer:
        if n % t == 0:
            return t
    raise ValueError(f"no tile for {n}")


def kernel(x, meta_tokens, norm_mix, w_in, conv_qk, b_igate, b_fgate_m, m_out_norm, b_fgate_f,
           f_q_norm, f_k_norm, w_out, norm_ffn, peer_query, peer_sub_keys, peer_u, peer_v):
    assert w_in.shape[0] == 1, "single-layer block"
    b, seq, d = x.shape
    t = FOX_BLOCK
    assert seq % t == 0 and N_META <= SEQ_BLOCK and t % SEQ_BLOCK == 0
    first_real = t - N_META
    lp = seq + t

    meta = jnp.broadcast_to(meta_tokens.astype(x.dtype)[None], (b, N_META, d))
    hpad = jnp.concatenate([jnp.zeros((b, first_real, d), x.dtype), meta, x], axis=1).reshape(b * lp, d)
    w = w_in[0]
    o = 0
    cols = {}
    for name, size in (("mq", M_WIDTH), ("mk", M_WIDTH), ("mv", M_WIDTH), ("mo", M_WIDTH), ("mi", M_HEADS),
                       ("mf", M_HEADS), ("fq", F_WIDTH), ("fk", F_WIDTH), ("fv", F_WIDTH), ("ff", F_HEADS)):
        cols[name] = w[:, o:o + size]
        o += size
    w_m = jnp.concatenate([cols["mq"], cols["mk"], cols["mv"], cols["mo"]], axis=1).astype(BF16)
    w_f = jnp.concatenate([cols["fq"], cols["fk"]], axis=1).astype(BF16)
    w_vt = cols["fv"].T.astype(BF16)
    n_gate = 2 * M_HEADS + F_HEADS
    w_g = jnp.concatenate([cols["mi"], cols["mf"], cols["ff"], jnp.zeros((d, GATE_LANES - n_gate), F32)], axis=1)
    w_gh, w_gl = _split_bf16(w_g)
    gate_bias = jnp.concatenate([b_igate[0], b_fgate_m[0], b_fgate_f[0],
                                 jnp.zeros((GATE_LANES - n_gate,), F32)]).reshape(1, GATE_LANES)

    zm, zf, vt, zg = _inproj(hpad, norm_mix[0].reshape(1, d), w_m, w_f, w_vt, w_gh, w_gl,
                             _tile(b * lp, (512, 256, 128)))
    gcol, grow = _gates(zg.reshape(b, lp, GATE_LANES), gate_bias)
    y_m = _mlstm(zm.reshape(b, lp, 4 * M_WIDTH), gcol, grow, conv_qk[0], m_out_norm[0].reshape(1, M_WIDTH), seq)
    y_f = _fox(zf.reshape(b, lp, 2 * F_WIDTH), vt, gcol, f_q_norm[0].reshape(1, F_HEAD_DIM),
               f_k_norm[0].reshape(1, F_HEAD_DIM), first_real)
    n = b * seq
    wo = w_out[0].astype(BF16)
    h2, xh, xl = _outproj(y_m.reshape(n, M_WIDTH), y_f.reshape(n, F_WIDTH), x.reshape(n, d),
                          wo[:M_WIDTH], wo[M_WIDTH:], norm_ffn[0].reshape(1, d), _tile(n, (512, 256, 128)))

    wqh, wql = _split_bf16(peer_query[0])
    kh, kl = _split_bf16(peer_sub_keys[0].reshape(2 * PEER_HEADS, N_KEYS, PEER_HALF))
    r2, e2, n1, e1 = _peer_route(xh, xl, wqh, wql, kh, kl, _tile(n, (256, 128)))
    u_packed = _pack_bf16(peer_u[0], False, 1024)
    vt_packed = _pack_bf16(peer_v[0], True, 1024)
    out = _peer_experts(xh, u_packed, vt_packed, r2, e2, n1, e1, h2, _tile(n, (512, 256, 128)), 16)
    return out.reshape(b, seq, d)
```

```python
import functools

import jax
import jax.numpy as jnp
from jax import lax
from jax.experimental import pallas as pl
from jax.experimental.pallas import tpu as pltpu

F32 = jnp.float32
BF16 = jnp.bfloat16

EPS = 1e-6
N_META = 16
M_HEADS = 4
M_HEAD_DIM = 128
M_WIDTH = M_HEADS * M_HEAD_DIM
CONV_W = 4
F_HEADS = 8
F_HEAD_DIM = 64
F_WIDTH = F_HEADS * F_HEAD_DIM
PEER_HEADS = 8
N_KEYS = 128
PEER_TOPK = 16
PEER_HALF = 128

SEQ_BLOCK = 128
FOX_BLOCK = 256
GATE_LANES = 128
NEG_BIG = -1e30
VMEM_LIMIT = 56 * 1024 * 1024

_NT = (((1,), (1,)), ((), ()))
_TN = (((0,), (0,)), ((), ()))


def _params(*sem, flags=None):
    return pltpu.CompilerParams(dimension_semantics=sem, vmem_limit_bytes=VMEM_LIMIT, flags=flags)


def _split_bf16(a):
    hi = a.astype(BF16)
    lo = (a - hi.astype(F32)).astype(BF16)
    return hi, lo


def _dot(a, b):
    return jnp.dot(a, b, preferred_element_type=F32)


def _dot3(ah, al, bh, bl, dims=None):
    if dims is None:
        f = _dot
    else:
        f = lambda x, y: lax.dot_general(x, y, dims, preferred_element_type=F32)
    return f(ah, bh) + (f(ah, bl) + f(al, bh))


def _inproj_kernel(h_ref, g_ref, wm_ref, wf_ref, fg_ref, wvt_ref, wgh_ref, wgl_ref, zm_ref, zf_ref, vt_ref, zg_ref):
    h = h_ref[...]
    hn = h * lax.rsqrt(jnp.mean(h * h, axis=-1, keepdims=True) + EPS) * g_ref[...]
    hb, hl = _split_bf16(hn)
    zm_ref[...] = _dot(hb, wm_ref[...])
    zf = _dot(hb, wf_ref[...])
    low = lax.broadcasted_iota(jnp.int32, (zf.shape[0], 128), 1) < F_HEAD_DIM
    for c in range(zf.shape[1] // 128):
        x = zf[:, c * 128:(c + 1) * 128]
        x2 = x * x
        ms_lo = jnp.sum(jnp.where(low, x2, 0.0), axis=1, keepdims=True) * (1.0 / F_HEAD_DIM)
        ms_hi = jnp.sum(jnp.where(low, 0.0, x2), axis=1, keepdims=True) * (1.0 / F_HEAD_DIM)
        r = jnp.where(low, lax.rsqrt(ms_lo + EPS), lax.rsqrt(ms_hi + EPS))
        zf_ref[:, c * 128:(c + 1) * 128] = x * r * fg_ref[:, c * 128:(c + 1) * 128]
    vt_ref[...] = lax.dot_general(wvt_ref[...], hb, _NT, preferred_element_type=F32)
    zg_ref[...] = _dot3(hb, hl, wgh_ref[...], wgl_ref[...])


def _inproj(h2d, g, wm, wf, fg, wvt, wgh, wgl, tm):
    n, d = h2d.shape
    full = lambda a: pl.BlockSpec(a.shape, lambda i: (0, 0))
    return pl.pallas_call(
        _inproj_kernel,
        grid=(n // tm,),
        in_specs=[pl.BlockSpec((tm, d), lambda i: (i, 0)), full(g), full(wm), full(wf), full(fg), full(wvt),
                  full(wgh), full(wgl)],
        out_specs=[pl.BlockSpec((tm, wm.shape[1]), lambda i: (i, 0)),
                   pl.BlockSpec((tm, wf.shape[1]), lambda i: (i, 0)),
                   pl.BlockSpec((wvt.shape[0], tm), lambda i: (0, i)),
                   pl.BlockSpec((tm, GATE_LANES), lambda i: (i, 0))],
        out_shape=[jax.ShapeDtypeStruct((n, wm.shape[1]), F32),
                   jax.ShapeDtypeStruct((n, wf.shape[1]), F32),
                   jax.ShapeDtypeStruct((wvt.shape[0], n), F32),
                   jax.ShapeDtypeStruct((n, GATE_LANES), F32)],
        compiler_params=_params("parallel"),
        name="inproj",
    )(h2d, g, wm, wf, fg, wvt, wgh, wgl)


def _log_sigmoid(x):
    return jnp.minimum(x, 0.0) - jnp.log1p(jnp.exp(-jnp.abs(x)))


def _gates_kernel(zg_ref, bias_ref, gcol_ref, grow_ref):
    lp = zg_ref.shape[0]
    t = SEQ_BLOCK
    r = lax.broadcasted_iota(jnp.int32, (t, t), 0)
    c = lax.broadcasted_iota(jnp.int32, (t, t), 1)
    tri = jnp.where(r >= c, 1.0, 0.0).astype(BF16)
    lane = lax.broadcasted_iota(jnp.int32, (t, GATE_LANES), 1)
    carry = jnp.zeros((1, GATE_LANES), F32)
    for i in range(lp // t):
        pre = zg_ref[i * t:(i + 1) * t, :] + bias_ref[...]
        ls = _log_sigmoid(pre)
        l0 = ls.astype(BF16)
        r1 = ls - l0.astype(F32)
        l1 = r1.astype(BF16)
        l2 = (r1 - l1.astype(F32)).astype(BF16)
        cs = (_dot(tri, l0) + (_dot(tri, l1) + _dot(tri, l2))) + carry
        carry = cs[t - 1:t, :]
        out = jnp.where(lane < M_HEADS, pre, cs)
        gcol_ref[i * t:(i + 1) * t, :] = out
        grow_ref[:, i * t:(i + 1) * t] = out.T[:16, :]


def _gates(zg, bias):
    b, lp, _ = zg.shape
    return pl.pallas_call(
        _gates_kernel,
        grid=(b,),
        in_specs=[pl.BlockSpec((None, lp, GATE_LANES), lambda i: (i, 0, 0)),
                  pl.BlockSpec((1, GATE_LANES), lambda i: (0, 0))],
        out_specs=[pl.BlockSpec((None, lp, GATE_LANES), lambda i: (i, 0, 0)),
                   pl.BlockSpec((None, 16, lp), lambda i: (i, 0, 0))],
        out_shape=[jax.ShapeDtypeStruct((b, lp, GATE_LANES), F32),
                   jax.ShapeDtypeStruct((b, 16, lp), F32)],
        compiler_params=_params("parallel"),
        name="gates",
    )(zg, bias)


def _mlstm_kernel(has_prev, zm_ref, gcol_ref, grow_ref, gprev_ref, convw_ref, gain_ref, y_ref,
                  prevx_ref, ct_ref, n_ref, m_ref, pc_ref):
    j = pl.program_id(1)
    t = SEQ_BLOCK
    w2 = 2 * M_WIDTH

    @pl.when(j == 0)
    def _():
        prevx_ref[...] = jnp.zeros_like(prevx_ref)
        ct_ref[...] = jnp.zeros_like(ct_ref)
        n_ref[...] = jnp.zeros_like(n_ref)
        m_ref[...] = jnp.zeros_like(m_ref)
        pc_ref[...] = (jnp.broadcast_to(gprev_ref[7:8, :], pc_ref.shape) if has_prev
                       else jnp.zeros_like(pc_ref))

    xqk = zm_ref[:, :w2]
    prev = prevx_ref[...]
    rowi = lax.broadcasted_iota(jnp.int32, (t, w2), 0)
    acc = xqk * convw_ref[CONV_W - 1:CONV_W, :]
    for s in range(1, CONV_W):
        shifted = jnp.where(rowi < s, pltpu.roll(prev, s, 0), pltpu.roll(xqk, s, 0))
        acc = acc + shifted * convw_ref[CONV_W - 1 - s:CONV_W - s, :]
    prevx_ref[...] = xqk
    qk = acc * jax.nn.sigmoid(acc)

    gcol = gcol_ref[...]
    grow = grow_ref[...]
    rr = lax.broadcasted_iota(jnp.int32, (t, t), 0)
    cc = lax.broadcasted_iota(jnp.int32, (t, t), 1)
    causal = rr >= cc
    scale = M_HEAD_DIM ** -0.5

    heads = range(M_HEADS)
    hsl = [slice(h * M_HEAD_DIM, (h + 1) * M_HEAD_DIM) for h in heads]
    q = [qk[:, hsl[h]] for h in heads]
    k = [qk[:, M_WIDTH + h * M_HEAD_DIM:M_WIDTH + (h + 1) * M_HEAD_DIM] * scale for h in heads]
    qb = [x.astype(BF16) for x in q]
    kb = [x.astype(BF16) for x in k]
    vb = [zm_ref[:, w2 + h * M_HEAD_DIM:w2 + (h + 1) * M_HEAD_DIM].astype(BF16) for h in heads]
    ct = [ct_ref[h] for h in heads]
    n_row = [n_ref[h, 0:1, :] for h in heads]
    m_prev = [m_ref[h, 0:1, 0:1] for h in heads]

    qk_dot = [lax.dot_general(qb[h], kb[h], _NT, preferred_element_type=F32) for h in heads]
    q_ct = [_dot(qb[h], ct[h].astype(BF16)) for h in heads]

    li_c = [gcol[:, h:h + 1] for h in heads]
    li_r = [grow[h:h + 1, :] for h in heads]
    prev_cum = [pc_ref[0:1, M_HEADS + h:M_HEADS + h + 1] for h in heads]
    b_c = [gcol[:, M_HEADS + h:M_HEADS + h + 1] - prev_cum[h] for h in heads]
    b_r = [grow[M_HEADS + h:M_HEADS + h + 1, :] - prev_cum[h] for h in heads]
    dm = [jnp.where(causal, (b_c[h] - b_r[h]) + li_r[h], -jnp.inf) for h in heads]
    m_inter = [b_c[h] + m_prev[h] for h in heads]
    m_t = [jnp.maximum(m_inter[h], jnp.max(dm[h], axis=1, keepdims=True)) for h in heads]
    w_inter = [jnp.exp(m_inter[h] - m_t[h]) for h in heads]
    s_mat = [qk_dot[h] * jnp.exp(dm[h] - m_t[h]) for h in heads]

    num = [_dot(s_mat[h].astype(BF16), vb[h]) + w_inter[h] * q_ct[h] for h in heads]
    den = [jnp.sum(s_mat[h], axis=1, keepdims=True)
           + w_inter[h] * jnp.sum(q[h] * n_row[h], axis=1, keepdims=True) for h in heads]
    hh = [num[h] / jnp.maximum(jnp.abs(den[h]), jnp.exp(-m_t[h])) for h in heads]

    b_end = [b_c[h][t - 1:t, :] for h in heads]
    g_c = [(b_end[h] - b_c[h]) + li_c[h] for h in heads]
    g_r = [(b_end[h] - b_r[h]) + li_r[h] for h in heads]
    m_new = [jnp.maximum(b_end[h] + m_prev[h], jnp.max(g_r[h], axis=1, keepdims=True)) for h in heads]
    decay = [jnp.exp(b_end[h] + m_prev[h] - m_new[h]) for h in heads]
    kw = [k[h] * jnp.exp(g_c[h] - m_new[h]) for h in heads]
    for h in heads:
        ct_ref[h] = decay[h] * ct[h] + lax.dot_general(kw[h].astype(BF16), vb[h], _TN, preferred_element_type=F32)
        n_ref[h] = jnp.broadcast_to(decay[h] * n_row[h] + jnp.sum(kw[h], axis=0, keepdims=True), (8, M_HEAD_DIM))
        m_ref[h] = jnp.broadcast_to(m_new[h], (8, 128))

    for h in heads:
        o_pre = zm_ref[:, w2 + M_WIDTH + h * M_HEAD_DIM:w2 + M_WIDTH + (h + 1) * M_HEAD_DIM]
        hn = hh[h] * lax.rsqrt(jnp.mean(hh[h] * hh[h], axis=-1, keepdims=True) + EPS) * gain_ref[:, hsl[h]]
        y_ref[:, hsl[h]] = (hn * jax.nn.sigmoid(o_pre)).astype(y_ref.dtype)

    pc_ref[...] = jnp.broadcast_to(gcol[t - 1:t, :], (8, GATE_LANES))


def _mlstm(zm, gcol, grow, convw, gain, seq):
    b, lp, wz = zm.shape
    t = SEQ_BLOCK
    nb = seq // t + 1
    skip = lp // t - nb
    prev_rows = max(skip * t // 8 - 1, 0)
    return pl.pallas_call(
        functools.partial(_mlstm_kernel, skip > 0),
        grid=(b, nb),
        in_specs=[pl.BlockSpec((None, t, wz), lambda i, j: (i, j + skip, 0)),
                  pl.BlockSpec((None, t, GATE_LANES), lambda i, j: (i, j + skip, 0)),
                  pl.BlockSpec((None, 16, t), lambda i, j: (i, 0, j + skip)),
                  pl.BlockSpec((None, 8, GATE_LANES), lambda i, j: (i, prev_rows, 0)),
                  pl.BlockSpec(convw.shape, lambda i, j: (0, 0)),
                  pl.BlockSpec(gain.shape, lambda i, j: (0, 0))],
        out_specs=pl.BlockSpec((None, t, M_WIDTH), lambda i, j: (i, jnp.maximum(j - 1, 0), 0)),
        out_shape=jax.ShapeDtypeStruct((b, seq, M_WIDTH), BF16),
        scratch_shapes=[pltpu.VMEM((t, 2 * M_WIDTH), F32),
                        pltpu.VMEM((M_HEADS, M_HEAD_DIM, M_HEAD_DIM), F32),
                        pltpu.VMEM((M_HEADS, 8, M_HEAD_DIM), F32),
                        pltpu.VMEM((M_HEADS, 8, 128), F32),
                        pltpu.VMEM((8, GATE_LANES), F32)],
        compiler_params=_params("parallel", "arbitrary"),
        name="mlstm",
    )(zm, gcol, grow, gcol, convw, gain)


_FOX_KDIM = 128
_LOG2E = 1.4426950408889634


def _split3(c):
    c0 = c.astype(BF16).astype(F32)
    r1 = c - c0
    c1 = r1.astype(BF16).astype(F32)
    c2 = (r1 - c1).astype(BF16).astype(F32)
    return c0, c1, c2


def _fox_kernel(first_real, q_ref, k_ref, v_ref, gcol_ref, y_ref,
                kx_ref, vt_ref, qx_ref, *state_refs):
    qi = pl.program_id(1) + 1
    t = FOX_BLOCK
    d = F_HEAD_DIM
    lp = k_ref.shape[0]
    elane = lax.broadcasted_iota(jnp.int32, (t, _FOX_KDIM - d), 1)

    @pl.when(qi == 1)
    def _():
        def prep(c, carry):
            r0 = pl.multiple_of(c * t, t)
            rows = pl.ds(r0, t)
            vt_ref[:, rows] = v_ref[:, rows].astype(BF16)
            g = gcol_ref[rows, :]
            for h in range(F_HEADS):
                kn = k_ref[rows, h * d:(h + 1) * d]
                c0, c1, c2 = _split3(g[:, 2 * M_HEADS + h:2 * M_HEADS + h + 1] * _LOG2E)
                extra = jnp.where(elane < 3, 1.0,
                                  jnp.where(elane == 3, -c0, jnp.where(elane == 4, -c1,
                                                                       jnp.where(elane == 5, -c2, 0.0))))
                kx_ref[rows, h * _FOX_KDIM:(h + 1) * _FOX_KDIM] = jnp.concatenate([kn, extra], axis=1).astype(BF16)
            return carry
        lax.fori_loop(0, lp // t, prep, 0)

    scale = d ** -0.5 * _LOG2E
    gq_rows = gcol_ref[pl.ds(pl.multiple_of(qi * t, t), t), :] * _LOG2E
    for h in range(F_HEADS):
        qn = q_ref[:, h * d:(h + 1) * d] * scale
        c0, c1, c2 = _split3(gq_rows[:, 2 * M_HEADS + h:2 * M_HEADS + h + 1])
        extra = jnp.where(elane == 0, c0, jnp.where(elane == 1, c1, jnp.where(elane == 2, c2,
                                                                              jnp.where(elane < 6, 1.0, 0.0))))
        qx_ref[h] = jnp.concatenate([qn, extra], axis=1).astype(BF16)

    acc_refs, m_refs, l_refs = (state_refs[i * F_HEADS:(i + 1) * F_HEADS] for i in range(3))
    for h in range(F_HEADS):
        m_refs[h][...] = jnp.full_like(m_refs[h], NEG_BIG)
        l_refs[h][...] = jnp.zeros_like(l_refs[h])
        acc_refs[h][...] = jnp.zeros_like(acc_refs[h])

    kk = lax.broadcasted_iota(jnp.int32, (t, t), 0)
    qq = lax.broadcasted_iota(jnp.int32, (t, t), 1)

    def step(jb, mask):
        start = pl.multiple_of(jb * t, t)
        scores = []
        for h in range(F_HEADS):
            kx = kx_ref[pl.ds(start, t), h * _FOX_KDIM:(h + 1) * _FOX_KDIM]
            scores.append(lax.dot_general(kx, qx_ref[h], _NT, preferred_element_type=F32))
        probs, alphas = [], []
        for h in range(F_HEADS):
            s = scores[h] if mask is None else jnp.where(mask, scores[h], NEG_BIG)
            m_old = m_refs[h][...]
            m_new = jnp.maximum(m_old, jnp.max(s, axis=0, keepdims=True))
            p = jnp.exp2(s - m_new)
            alpha = jnp.exp2(m_old - m_new)
            l_refs[h][...] = alpha * l_refs[h][...] + jnp.sum(p, axis=0, keepdims=True)
            m_refs[h][...] = m_new
            probs.append(p.astype(BF16))
            alphas.append(alpha)
        for h in range(F_HEADS):
            acc_refs[h][...] = alphas[h] * acc_refs[h][...] + _dot(vt_ref[h * d:(h + 1) * d, pl.ds(start, t)],
                                                                  probs[h])

    step(0, kk >= first_real)
    lax.fori_loop(1, qi, lambda jb, c: (step(jb, None), c)[1], 0)
    step(qi, kk <= qq)
    yt = jnp.concatenate([acc_refs[h][...] / l_refs[h][...] for h in range(F_HEADS)], axis=0)
    y_ref[...] = yt.T.astype(y_ref.dtype)


def _fox(zf, vt, gcol, first_real):
    b, lp, _ = zf.shape
    t = FOX_BLOCK
    nb = lp // t
    return pl.pallas_call(
        functools.partial(_fox_kernel, first_real),
        grid=(b, nb - 1),
        in_specs=[pl.BlockSpec((None, t, F_WIDTH), lambda i, j: (i, j + 1, 0)),
                  pl.BlockSpec((None, lp, F_WIDTH), lambda i, j: (i, 0, 1)),
                  pl.BlockSpec((F_WIDTH, lp), lambda i, j: (0, i)),
                  pl.BlockSpec((None, lp, GATE_LANES), lambda i, j: (i, 0, 0))],
        out_specs=pl.BlockSpec((None, t, F_WIDTH), lambda i, j: (i, j, 0)),
        out_shape=jax.ShapeDtypeStruct((b, lp - t, F_WIDTH), BF16),
        scratch_shapes=[pltpu.VMEM((lp, F_HEADS * _FOX_KDIM), BF16),
                        pltpu.VMEM((F_WIDTH, lp), BF16),
                        pltpu.VMEM((F_HEADS, t, _FOX_KDIM), BF16)]
                       + [pltpu.VMEM((F_HEAD_DIM, t), F32)] * F_HEADS
                       + [pltpu.VMEM((1, t), F32)] * F_HEADS
                       + [pltpu.VMEM((1, t), F32)] * F_HEADS,
        compiler_params=_params("parallel", "arbitrary"),
        name="fox",
    )(zf, zf, vt, gcol)


def _outproj_kernel(ym_ref, yf_ref, x_ref, wm_ref, wf_ref, g_ref, h2_ref, xn_ref):
    h2 = x_ref[...] + (_dot(ym_ref[...], wm_ref[...]) + _dot(yf_ref[...], wf_ref[...]))
    h2_ref[...] = h2
    xn = h2 * lax.rsqrt(jnp.mean(h2 * h2, axis=-1, keepdims=True) + EPS) * g_ref[...]
    xn_ref[...] = xn.astype(BF16)


def _outproj(ym, yf, x2d, wm, wf, g, tm):
    n, d = x2d.shape
    row = lambda w: pl.BlockSpec((tm, w), lambda i: (i, 0))
    full = lambda a: pl.BlockSpec(a.shape, lambda i: (0, 0))
    return pl.pallas_call(
        _outproj_kernel,
        grid=(n // tm,),
        in_specs=[row(ym.shape[1]), row(yf.shape[1]), row(d), full(wm), full(wf), full(g)],
        out_specs=[row(d), row(d)],
        out_shape=[jax.ShapeDtypeStruct((n, d), F32),
                   jax.ShapeDtypeStruct((n, d), BF16)],
        compiler_params=_params("parallel"),
        name="outproj",
    )(ym, yf, x2d, wm, wf, g)


_TOP = PEER_TOPK + 1
_TOP_ROWS = 24


def _merge_exchange_network(n):
    t = max(1, (n - 1).bit_length())
    pairs = []
    p = 1 << (t - 1)
    while p > 0:
        q, r, d = 1 << (t - 1), 0, p
        while d > 0:
            pairs += [(i, i + d) for i in range(n - d) if (i & p) == r]
            d, q, r = q - p, q >> 1, p
        p >>= 1
    return pairs


def _pop_largest(s):
    m = s.shape[0] // 8
    cols = [s[8 * i:8 * (i + 1), :] for i in range(m)]
    for i, j in _merge_exchange_network(m):
        cols[i], cols[j] = jnp.maximum(cols[i], cols[j]), jnp.minimum(cols[i], cols[j])
    values, counts = [], []
    for it in range(_TOP):
        best = jnp.max(cols[0], axis=0, keepdims=True)
        eq = cols[0] == best
        values.append(best)
        counts.append(jnp.sum(jnp.where(eq, 1.0, 0.0), axis=0, keepdims=True))
        live = min(m, _TOP - it)
        for d in range(live - 1):
            cols[d] = jnp.where(eq, cols[d + 1], cols[d])
        if live == m:
            cols[m - 1] = jnp.where(eq, -jnp.inf, cols[m - 1])
    return values, counts


def _top_values(s):
    tb = s.shape[1]
    rank = lax.broadcasted_iota(jnp.int32, (_TOP_ROWS, tb), 0).astype(F32)
    vals = None
    cnt = jnp.zeros((1, tb), F32)
    for best, k in zip(*_pop_largest(s)):
        vals = jnp.where(rank >= cnt, best, -jnp.inf if vals is None else vals)
        cnt = cnt + k
    return jnp.where(rank < _TOP, vals, -jnp.inf)


def _pair_threshold(a, b):
    k = PEER_TOPK
    slabs = [a[0:1, :] + b[0:_TOP_ROWS, :]]
    for r in range(1, 8):
        slabs.append(a[r:r + 1, :] + b[0:8, :])
    slabs.append(a[8:_TOP_ROWS, :] + b[0:1, :])
    top = a[0:1, :] + b[0:1, :]
    cnt = jnp.zeros_like(top)
    v_k = top
    v_k1 = top
    z = jnp.zeros_like(top)
    for best, n_eq in zip(*_pop_largest(jnp.concatenate(slabs, axis=0))):
        take = jnp.clip(k - cnt, 0.0, n_eq)
        z = z + take * jnp.exp(best - top)
        v_k = jnp.where(cnt < k, best, v_k)
        v_k1 = jnp.where(cnt < k + 1, best, v_k1)
        cnt = cnt + n_eq
    return 0.5 * (v_k + v_k1), top, z


def _count_leading(sorted_rows, pred):
    out = None
    for c in range(_TOP):
        hit = pred(sorted_rows[c:c + 1, :])
        out = jnp.where(hit, c + 1.0, 0.0 if out is None else out)
    return out


def _peer_route_kernel(xn_ref, wq_ref, kh_ref, kl_ref, r2_ref, e2_ref, n_ref, e1_ref):
    q = _dot(xn_ref[...], wq_ref[...])
    for h in range(PEER_HEADS):
        st = []
        for p in range(2):
            i = 2 * h + p
            qh, ql = _split_bf16(q[:, i * PEER_HALF:(i + 1) * PEER_HALF])
            st.append(_dot3(kh_ref[i], kl_ref[i], qh, ql, _NT))
        a = _top_values(st[0])
        b = _top_values(st[1])
        thr, top, z = _pair_threshold(a, b)
        need = thr - st[0]
        above2 = _count_leading(b, lambda row: row > st[1])
        n1 = _count_leading(b, lambda row: row >= need)
        r2_ref[h] = pltpu.bitcast(above2.astype(BF16), jnp.uint32)
        e2_ref[h] = pltpu.bitcast(jnp.exp(st[1] - b[0:1, :]).astype(BF16), jnp.uint32)
        n_ref[h] = n1
        e1_ref[h] = jnp.exp(st[0] - a[0:1, :]) * (0.5 / z)


def _peer_route(xn, wq, kh, kl, tb):
    n, d = xn.shape
    row = pl.BlockSpec((tb, d), lambda i: (i, 0))
    full2 = lambda a: pl.BlockSpec(a.shape, lambda i: (0, 0))
    full3 = lambda a: pl.BlockSpec(a.shape, lambda i: (0, 0, 0))
    out = lambda rows: pl.BlockSpec((PEER_HEADS, rows, tb), lambda i: (0, 0, i))
    shape = lambda rows, dt: jax.ShapeDtypeStruct((PEER_HEADS, rows, n), dt)
    return pl.pallas_call(
        _peer_route_kernel,
        grid=(n // tb,),
        in_specs=[row, full2(wq), full3(kh), full3(kl)],
        out_specs=[out(N_KEYS // 2), out(N_KEYS // 2), out(N_KEYS), out(N_KEYS)],
        out_shape=[shape(N_KEYS // 2, jnp.uint32), shape(N_KEYS // 2, jnp.uint32),
                   shape(N_KEYS, F32), shape(N_KEYS, F32)],
        compiler_params=_params("parallel"),
        name="peer_route",
    )(xn, wq, kh, kl)


def _twice_gelu(x):
    return x * (1.0 + lax.erf(x * (2.0 ** -0.5)))


_PACK = 16


_ROWS_PER_PIECE = 2
_PIECES_PER_CHUNK = 2


def _pack_rows_kernel(x_ref, o_ref):
    o_ref[...] = pltpu.bitcast(x_ref[...].astype(BF16), jnp.uint32)


def _pack_rows_t_kernel(x_ref, o_ref):
    o_ref[...] = pltpu.bitcast(x_ref[...].T.astype(BF16), jnp.uint32)


def _pack_bf16(x, transpose, rows):
    n, c = x.shape
    if transpose:
        kern, out_shape, out_spec = _pack_rows_t_kernel, (c // 2, n), pl.BlockSpec((c // 2, rows), lambda i: (0, i))
    else:
        kern, out_shape, out_spec = _pack_rows_kernel, (n // 2, c), pl.BlockSpec((rows // 2, c), lambda i: (i, 0))
    return pl.pallas_call(
        kern,
        grid=(n // rows,),
        in_specs=[pl.BlockSpec((rows, c), lambda i: (i, 0))],
        out_specs=out_spec,
        out_shape=jax.ShapeDtypeStruct(out_shape, jnp.uint32),
        compiler_params=_params("parallel"),
        name="pack_t" if transpose else "pack",
    )(x)


def _peer_expert_kernel(ib, xn_ref, u_ref, vt_ref, r2_ref, e2_ref, n_ref, e1_ref, h2_ref, o_ref,
                        acc_ref, w_ref, act_ref):
    e = pl.program_id(1)
    tb = xn_ref.shape[0]
    d = 2 * vt_ref.shape[0]
    piece = _ROWS_PER_PIECE * N_KEYS
    chunk = _PIECES_PER_CHUNK * piece
    n_chunks = ib * N_KEYS // chunk
    out_rows = d // _PIECES_PER_CHUNK

    @pl.when(e == 0)
    def _():
        acc_ref[...] = jnp.zeros_like(acc_ref)

    def act_piece(c, k):
        lo = c * chunk + k * piece
        act_ref[c % 2, k * piece:(k + 1) * piece, :] = lax.dot_general(
            pltpu.bitcast(u_ref[lo // 2:(lo + piece) // 2, :], BF16), xn_ref[...], _NT,
            preferred_element_type=F32)

    def out_piece(c, k):
        rows = slice(k * out_rows, (k + 1) * out_rows)
        vt = pltpu.bitcast(vt_ref[k * out_rows // 2:(k + 1) * out_rows // 2, c * chunk:(c + 1) * chunk], BF16)
        acc_ref[rows, :] += _dot(vt, w_ref[c * chunk:(c + 1) * chunk, :])

    def gate_piece(c, k):
        i0 = (c * chunk + k * piece) // N_KEYS
        for tc in range(tb // 128):
            cols = slice(tc * 128, (tc + 1) * 128)
            bcast = lambda ref, h, i: jnp.broadcast_to(ref[h, i:i + 1, cols], (_PACK, 128)).astype(BF16)
            gates = [[None] * (N_KEYS // _PACK) for _ in range(_ROWS_PER_PIECE)]
            for h in range(PEER_HEADS):
                n1 = [bcast(n_ref, h, i0 + r) for r in range(_ROWS_PER_PIECE)]
                e1 = [bcast(e1_ref, h, i0 + r) for r in range(_ROWS_PER_PIECE)]
                for g in range(N_KEYS // _PACK):
                    ks = slice(g * _PACK // 2, (g + 1) * _PACK // 2)
                    above2 = pltpu.bitcast(r2_ref[h, ks, cols], BF16)
                    e2 = pltpu.bitcast(e2_ref[h, ks, cols], BF16)
                    for r in range(_ROWS_PER_PIECE):
                        term = e1[r] * jnp.where(above2 < n1[r], e2, jnp.zeros_like(e2))
                        gates[r][g] = term if gates[r][g] is None else gates[r][g] + term
            for r in range(_ROWS_PER_PIECE):
                for g in range(N_KEYS // _PACK):
                    lo = k * piece + r * N_KEYS + g * _PACK
                    act = act_ref[c % 2, lo:lo + _PACK, cols]
                    w_ref[c * chunk + lo:c * chunk + lo + _PACK, cols] = gates[r][g] * _twice_gelu(act).astype(BF16)

    for k in range(_PIECES_PER_CHUNK):
        act_piece(0, k)
    for c in range(n_chunks):
        for k in range(_PIECES_PER_CHUNK):
            if c + 1 < n_chunks:
                act_piece(c + 1, k)
            if c > 0:
                out_piece(c - 1, k)
            gate_piece(c, k)
    for k in range(_PIECES_PER_CHUNK):
        out_piece(n_chunks - 1, k)

    @pl.when(e == pl.num_programs(1) - 1)
    def _():
        o_ref[...] = h2_ref[...] + acc_ref[...].T


def _peer_experts(xn, u_packed, vt_packed, r2, e2, n1, e1, h2, tb, ib):
    n, d = xn.shape
    ne = 2 * u_packed.shape[0]
    eb = ib * N_KEYS
    chunk = _PIECES_PER_CHUNK * _ROWS_PER_PIECE * N_KEYS
    assert eb % chunk == 0 and tb % 128 == 0
    gate_full = pl.BlockSpec((PEER_HEADS, N_KEYS // 2, tb), lambda t, e: (0, 0, t))
    gate_rows = pl.BlockSpec((PEER_HEADS, ib, tb), lambda t, e: (0, e, t))
    return pl.pallas_call(
        functools.partial(_peer_expert_kernel, ib),
        grid=(n // tb, ne // eb),
        in_specs=[pl.BlockSpec((tb, d), lambda t, e: (t, 0)),
                  pl.BlockSpec((eb // 2, d), lambda t, e: (e, 0)),
                  pl.BlockSpec((d // 2, eb), lambda t, e: (0, e)),
                  gate_full, gate_full, gate_rows, gate_rows,
                  pl.BlockSpec((tb, d), lambda t, e: (t, 0))],
        out_specs=pl.BlockSpec((tb, d), lambda t, e: (t, 0)),
        out_shape=jax.ShapeDtypeStruct((n, d), F32),
        scratch_shapes=[pltpu.VMEM((d, tb), F32),
                        pltpu.VMEM((eb, tb), BF16),
                        pltpu.VMEM((2, chunk, tb), F32)],
        compiler_params=_params("parallel", "arbitrary"),
        name="peer_experts",
    )(xn, u_packed, vt_packed, r2, e2, n1, e1, h2)


def _tile(n, prefer):
    for t in prefer:
        if n % t == 0:
            return t
    raise ValueError(f"no tile for {n}")


def kernel(x, meta_tokens, norm_mix, w_in, conv_qk, b_igate, b_fgate_m, m_out_norm, b_fgate_f,
           f_q_norm, f_k_norm, w_out, norm_ffn, peer_query, peer_sub_keys, peer_u, peer_v):
    assert w_in.shape[0] == 1, "single-layer block"
    b, seq, d = x.shape
    t = FOX_BLOCK
    assert seq % t == 0 and N_META <= SEQ_BLOCK and t % SEQ_BLOCK == 0
    first_real = t - N_META
    lp = seq + t

    meta = jnp.broadcast_to(meta_tokens.astype(x.dtype)[None], (b, N_META, d))
    hpad = jnp.concatenate([jnp.zeros((b, first_real, d), x.dtype), meta, x], axis=1).reshape(b * lp, d)
    w = w_in[0]
    o = 0
    cols = {}
    for name, size in (("mq", M_WIDTH), ("mk", M_WIDTH), ("mv", M_WIDTH), ("mo", M_WIDTH), ("mi", M_HEADS),
                       ("mf", M_HEADS), ("fq", F_WIDTH), ("fk", F_WIDTH), ("fv", F_WIDTH), ("ff", F_HEADS)):
        cols[name] = w[:, o:o + size]
        o += size
    w_m = jnp.concatenate([cols["mq"], cols["mk"], cols["mv"], cols["mo"]], axis=1).astype(BF16)
    w_f = jnp.concatenate([cols["fq"], cols["fk"]], axis=1).astype(BF16)
    w_vt = cols["fv"].T.astype(BF16)
    n_gate = 2 * M_HEADS + F_HEADS
    w_g = jnp.concatenate([cols["mi"], cols["mf"], cols["ff"], jnp.zeros((d, GATE_LANES - n_gate), F32)], axis=1)
    w_gh, w_gl = _split_bf16(w_g)
    gate_bias = jnp.concatenate([b_igate[0], b_fgate_m[0], b_fgate_f[0],
                                 jnp.zeros((GATE_LANES - n_gate,), F32)]).reshape(1, GATE_LANES)

    f_gain = jnp.concatenate([jnp.tile(f_q_norm[0], F_HEADS), jnp.tile(f_k_norm[0], F_HEADS)]).reshape(1, 2 * F_WIDTH)
    zm, zf, vt, zg = _inproj(hpad, norm_mix[0].reshape(1, d), w_m, w_f, f_gain, w_vt, w_gh, w_gl,
                             _tile(b * lp, (512, 256, 128)))
    gcol, grow = _gates(zg.reshape(b, lp, GATE_LANES), gate_bias)
    y_m = _mlstm(zm.reshape(b, lp, 4 * M_WIDTH), gcol, grow, conv_qk[0], m_out_norm[0].reshape(1, M_WIDTH), seq)
    y_f = _fox(zf.reshape(b, lp, 2 * F_WIDTH), vt, gcol, first_real)
    n = b * seq
    wo = w_out[0].astype(BF16)
    h2, xh = _outproj(y_m.reshape(n, M_WIDTH), y_f.reshape(n, F_WIDTH), x.reshape(n, d),
                      wo[:M_WIDTH], wo[M_WIDTH:], norm_ffn[0].reshape(1, d), _tile(n, (512, 256, 128)))

    kh, kl = _split_bf16(peer_sub_keys[0].reshape(2 * PEER_HEADS, N_KEYS, PEER_HALF))
    r2, e2, n1, e1 = _peer_route(xh, peer_query[0].astype(BF16), kh, kl, _tile(n, (256, 128)))
    u_packed = _pack_bf16(peer_u[0], False, 1024)
    vt_packed = _pack_bf16(peer_v[0], True, 1024)
    out = _peer_experts(xh, u_packed, vt_packed, r2, e2, n1, e1, h2, _tile(n, (512, 256, 128)), 16)
    return out.reshape(b, seq, d)
```

```python
import functools

import jax
import jax.numpy as jnp
from jax import lax
from jax.experimental import pallas as pl
from jax.experimental.pallas import tpu as pltpu

F32 = jnp.float32
BF16 = jnp.bfloat16

EPS = 1e-6
N_META = 16
M_HEADS = 4
M_HEAD_DIM = 128
M_WIDTH = M_HEADS * M_HEAD_DIM
CONV_W = 4
F_HEADS = 8
F_HEAD_DIM = 64
F_WIDTH = F_HEADS * F_HEAD_DIM
PEER_HEADS = 8
N_KEYS = 128
PEER_TOPK = 16
PEER_HALF = 128

SEQ_BLOCK = 128
FOX_BLOCK = 256
GATE_LANES = 128
NEG_BIG = -1e30
VMEM_LIMIT = 56 * 1024 * 1024

_NT = (((1,), (1,)), ((), ()))
_TN = (((0,), (0,)), ((), ()))


def _params(*sem, flags=None):
    return pltpu.CompilerParams(dimension_semantics=sem, vmem_limit_bytes=VMEM_LIMIT, flags=flags)


def _split_bf16(a):
    hi = a.astype(BF16)
    lo = (a - hi.astype(F32)).astype(BF16)
    return hi, lo


def _dot(a, b):
    return jnp.dot(a, b, preferred_element_type=F32)


def _dot3(ah, al, bh, bl, dims=None):
    if dims is None:
        f = _dot
    else:
        f = lambda x, y: lax.dot_general(x, y, dims, preferred_element_type=F32)
    return f(ah, bh) + (f(ah, bl) + f(al, bh))


def _inproj_kernel(x_ref, head_ref, g_ref, wm_ref, wf_ref, fg_ref, wvt_ref, wgh_ref, wgl_ref,
                   zm_ref, zf_ref, vt_ref, zg_ref):
    h = jnp.where(pl.program_id(1) == 0, head_ref[...], x_ref[...])
    hn = h * lax.rsqrt(jnp.mean(h * h, axis=-1, keepdims=True) + EPS) * g_ref[...]
    hb, hl = _split_bf16(hn)
    zm_ref[...] = _dot(hb, wm_ref[...])
    zf = _dot(hb, wf_ref[...])
    low = lax.broadcasted_iota(jnp.int32, (zf.shape[0], 128), 1) < F_HEAD_DIM
    for c in range(zf.shape[1] // 128):
        x = zf[:, c * 128:(c + 1) * 128]
        x2 = x * x
        ms_lo = jnp.sum(jnp.where(low, x2, 0.0), axis=1, keepdims=True) * (1.0 / F_HEAD_DIM)
        ms_hi = jnp.sum(jnp.where(low, 0.0, x2), axis=1, keepdims=True) * (1.0 / F_HEAD_DIM)
        r = jnp.where(low, lax.rsqrt(ms_lo + EPS), lax.rsqrt(ms_hi + EPS))
        zf_ref[:, c * 128:(c + 1) * 128] = x * r * fg_ref[:, c * 128:(c + 1) * 128]
    vt_ref[...] = lax.dot_general(wvt_ref[...], hb, _NT, preferred_element_type=F32)
    zg_ref[...] = _dot3(hb, hl, wgh_ref[...], wgl_ref[...])


def _inproj(x, head, g, wm, wf, fg, wvt, wgh, wgl):
    b, seq, d = x.shape
    tm = head.shape[0]
    nb = seq // tm + 1
    lp = nb * tm
    full = lambda a: pl.BlockSpec(a.shape, lambda i, j: (0, 0))
    rows = lambda w: pl.BlockSpec((None, tm, w), lambda i, j: (i, j, 0))
    return pl.pallas_call(
        _inproj_kernel,
        grid=(b, nb),
        in_specs=[pl.BlockSpec((None, tm, d), lambda i, j: (i, jnp.maximum(j - 1, 0), 0)), full(head), full(g),
                  full(wm), full(wf), full(fg), full(wvt), full(wgh), full(wgl)],
        out_specs=[rows(wm.shape[1]), rows(wf.shape[1]),
                   pl.BlockSpec((wvt.shape[0], tm), lambda i, j: (0, i * nb + j)),
                   rows(GATE_LANES)],
        out_shape=[jax.ShapeDtypeStruct((b, lp, wm.shape[1]), F32),
                   jax.ShapeDtypeStruct((b, lp, wf.shape[1]), F32),
                   jax.ShapeDtypeStruct((wvt.shape[0], b * lp), F32),
                   jax.ShapeDtypeStruct((b, lp, GATE_LANES), F32)],
        compiler_params=_params("parallel", "arbitrary"),
        name="inproj",
    )(x, head, g, wm, wf, fg, wvt, wgh, wgl)


def _log_sigmoid(x):
    return jnp.minimum(x, 0.0) - jnp.log1p(jnp.exp(-jnp.abs(x)))


def _gates_kernel(zg_ref, bias_ref, gcol_ref, grow_ref):
    lp = zg_ref.shape[0]
    t = SEQ_BLOCK
    r = lax.broadcasted_iota(jnp.int32, (t, t), 0)
    c = lax.broadcasted_iota(jnp.int32, (t, t), 1)
    tri = jnp.where(r >= c, 1.0, 0.0).astype(BF16)
    lane = lax.broadcasted_iota(jnp.int32, (t, GATE_LANES), 1)
    carry = jnp.zeros((1, GATE_LANES), F32)
    for i in range(lp // t):
        pre = zg_ref[i * t:(i + 1) * t, :] + bias_ref[...]
        ls = _log_sigmoid(pre)
        l0 = ls.astype(BF16)
        r1 = ls - l0.astype(F32)
        l1 = r1.astype(BF16)
        l2 = (r1 - l1.astype(F32)).astype(BF16)
        cs = (_dot(tri, l0) + (_dot(tri, l1) + _dot(tri, l2))) + carry
        carry = cs[t - 1:t, :]
        out = jnp.where(lane < M_HEADS, pre, cs)
        gcol_ref[i * t:(i + 1) * t, :] = out
        grow_ref[:, i * t:(i + 1) * t] = out.T[:16, :]


def _gates(zg, bias):
    b, lp, _ = zg.shape
    return pl.pallas_call(
        _gates_kernel,
        grid=(b,),
        in_specs=[pl.BlockSpec((None, lp, GATE_LANES), lambda i: (i, 0, 0)),
                  pl.BlockSpec((1, GATE_LANES), lambda i: (0, 0))],
        out_specs=[pl.BlockSpec((None, lp, GATE_LANES), lambda i: (i, 0, 0)),
                   pl.BlockSpec((None, 16, lp), lambda i: (i, 0, 0))],
        out_shape=[jax.ShapeDtypeStruct((b, lp, GATE_LANES), F32),
                   jax.ShapeDtypeStruct((b, 16, lp), F32)],
        compiler_params=_params("parallel"),
        name="gates",
    )(zg, bias)


def _mlstm_kernel(has_prev, zm_ref, gcol_ref, grow_ref, gprev_ref, convw_ref, gain_ref, y_ref,
                  prevx_ref, ct_ref, n_ref, m_ref, pc_ref):
    j = pl.program_id(1)
    t = SEQ_BLOCK
    w2 = 2 * M_WIDTH

    @pl.when(j == 0)
    def _():
        prevx_ref[...] = jnp.zeros_like(prevx_ref)
        ct_ref[...] = jnp.zeros_like(ct_ref)
        n_ref[...] = jnp.zeros_like(n_ref)
        m_ref[...] = jnp.zeros_like(m_ref)
        pc_ref[...] = (jnp.broadcast_to(gprev_ref[7:8, :], pc_ref.shape) if has_prev
                       else jnp.zeros_like(pc_ref))

    xqk = zm_ref[:, :w2]
    prev = prevx_ref[...]
    rowi = lax.broadcasted_iota(jnp.int32, (t, w2), 0)
    acc = xqk * convw_ref[CONV_W - 1:CONV_W, :]
    for s in range(1, CONV_W):
        shifted = jnp.where(rowi < s, pltpu.roll(prev, s, 0), pltpu.roll(xqk, s, 0))
        acc = acc + shifted * convw_ref[CONV_W - 1 - s:CONV_W - s, :]
    prevx_ref[...] = xqk
    qk = acc * jax.nn.sigmoid(acc)

    gcol = gcol_ref[...]
    grow = grow_ref[...]
    rr = lax.broadcasted_iota(jnp.int32, (t, t), 0)
    cc = lax.broadcasted_iota(jnp.int32, (t, t), 1)
    causal = rr >= cc
    scale = M_HEAD_DIM ** -0.5

    heads = range(M_HEADS)
    hsl = [slice(h * M_HEAD_DIM, (h + 1) * M_HEAD_DIM) for h in heads]
    q = [qk[:, hsl[h]] for h in heads]
    k = [qk[:, M_WIDTH + h * M_HEAD_DIM:M_WIDTH + (h + 1) * M_HEAD_DIM] * scale for h in heads]
    qb = [x.astype(BF16) for x in q]
    kb = [x.astype(BF16) for x in k]
    vb = [zm_ref[:, w2 + h * M_HEAD_DIM:w2 + (h + 1) * M_HEAD_DIM].astype(BF16) for h in heads]
    ct = [ct_ref[h] for h in heads]
    n_row = [n_ref[h, 0:1, :] for h in heads]
    m_prev = [m_ref[h, 0:1, 0:1] for h in heads]

    qk_dot = [lax.dot_general(qb[h], kb[h], _NT, preferred_element_type=F32) for h in heads]
    q_ct = [_dot(qb[h], ct[h].astype(BF16)) for h in heads]

    li_c = [gcol[:, h:h + 1] for h in heads]
    li_r = [grow[h:h + 1, :] for h in heads]
    prev_cum = [pc_ref[0:1, M_HEADS + h:M_HEADS + h + 1] for h in heads]
    b_c = [gcol[:, M_HEADS + h:M_HEADS + h + 1] - prev_cum[h] for h in heads]
    b_r = [grow[M_HEADS + h:M_HEADS + h + 1, :] - prev_cum[h] for h in heads]
    dm = [jnp.where(causal, (b_c[h] - b_r[h]) + li_r[h], -jnp.inf) for h in heads]
    m_inter = [b_c[h] + m_prev[h] for h in heads]
    m_t = [jnp.maximum(m_inter[h], jnp.max(dm[h], axis=1, keepdims=True)) for h in heads]
    w_inter = [jnp.exp(m_inter[h] - m_t[h]) for h in heads]
    s_mat = [qk_dot[h] * jnp.exp(dm[h] - m_t[h]) for h in heads]

    num = [_dot(s_mat[h].astype(BF16), vb[h]) + w_inter[h] * q_ct[h] for h in heads]
    den = [jnp.sum(s_mat[h], axis=1, keepdims=True)
           + w_inter[h] * jnp.sum(q[h] * n_row[h], axis=1, keepdims=True) for h in heads]
    hh = [num[h] / jnp.maximum(jnp.abs(den[h]), jnp.exp(-m_t[h])) for h in heads]

    b_end = [b_c[h][t - 1:t, :] for h in heads]
    g_c = [(b_end[h] - b_c[h]) + li_c[h] for h in heads]
    g_r = [(b_end[h] - b_r[h]) + li_r[h] for h in heads]
    m_new = [jnp.maximum(b_end[h] + m_prev[h], jnp.max(g_r[h], axis=1, keepdims=True)) for h in heads]
    decay = [jnp.exp(b_end[h] + m_prev[h] - m_new[h]) for h in heads]
    kw = [k[h] * jnp.exp(g_c[h] - m_new[h]) for h in heads]
    for h in heads:
        ct_ref[h] = decay[h] * ct[h] + lax.dot_general(kw[h].astype(BF16), vb[h], _TN, preferred_element_type=F32)
        n_ref[h] = jnp.broadcast_to(decay[h] * n_row[h] + jnp.sum(kw[h], axis=0, keepdims=True), (8, M_HEAD_DIM))
        m_ref[h] = jnp.broadcast_to(m_new[h], (8, 128))

    for h in heads:
        o_pre = zm_ref[:, w2 + M_WIDTH + h * M_HEAD_DIM:w2 + M_WIDTH + (h + 1) * M_HEAD_DIM]
        hn = hh[h] * lax.rsqrt(jnp.mean(hh[h] * hh[h], axis=-1, keepdims=True) + EPS) * gain_ref[:, hsl[h]]
        y_ref[:, hsl[h]] = (hn * jax.nn.sigmoid(o_pre)).astype(y_ref.dtype)

    pc_ref[...] = jnp.broadcast_to(gcol[t - 1:t, :], (8, GATE_LANES))


def _mlstm(zm, gcol, grow, convw, gain, seq):
    b, lp, wz = zm.shape
    t = SEQ_BLOCK
    nb = seq // t + 1
    skip = lp // t - nb
    prev_rows = max(skip * t // 8 - 1, 0)
    return pl.pallas_call(
        functools.partial(_mlstm_kernel, skip > 0),
        grid=(b, nb),
        in_specs=[pl.BlockSpec((None, t, wz), lambda i, j: (i, j + skip, 0)),
                  pl.BlockSpec((None, t, GATE_LANES), lambda i, j: (i, j + skip, 0)),
                  pl.BlockSpec((None, 16, t), lambda i, j: (i, 0, j + skip)),
                  pl.BlockSpec((None, 8, GATE_LANES), lambda i, j: (i, prev_rows, 0)),
                  pl.BlockSpec(convw.shape, lambda i, j: (0, 0)),
                  pl.BlockSpec(gain.shape, lambda i, j: (0, 0))],
        out_specs=pl.BlockSpec((None, t, M_WIDTH), lambda i, j: (i, jnp.maximum(j - 1, 0), 0)),
        out_shape=jax.ShapeDtypeStruct((b, seq, M_WIDTH), BF16),
        scratch_shapes=[pltpu.VMEM((t, 2 * M_WIDTH), F32),
                        pltpu.VMEM((M_HEADS, M_HEAD_DIM, M_HEAD_DIM), F32),
                        pltpu.VMEM((M_HEADS, 8, M_HEAD_DIM), F32),
                        pltpu.VMEM((M_HEADS, 8, 128), F32),
                        pltpu.VMEM((8, GATE_LANES), F32)],
        compiler_params=_params("parallel", "arbitrary"),
        name="mlstm",
    )(zm, gcol, grow, gcol, convw, gain)


_FOX_KDIM = 128
_LOG2E = 1.4426950408889634


def _split3(c):
    c0 = c.astype(BF16).astype(F32)
    r1 = c - c0
    c1 = r1.astype(BF16).astype(F32)
    c2 = (r1 - c1).astype(BF16).astype(F32)
    return c0, c1, c2


def _fox_kernel(first_real, q_ref, k_ref, v_ref, gcol_ref, y_ref,
                kx_ref, vt_ref, qx_ref, *state_refs):
    qi = pl.program_id(1) + 1
    t = FOX_BLOCK
    d = F_HEAD_DIM
    lp = k_ref.shape[0]
    elane = lax.broadcasted_iota(jnp.int32, (t, _FOX_KDIM - d), 1)

    @pl.when(qi == 1)
    def _():
        def prep(c, carry):
            r0 = pl.multiple_of(c * t, t)
            rows = pl.ds(r0, t)
            vt_ref[:, rows] = v_ref[:, rows].astype(BF16)
            g = gcol_ref[rows, :]
            for h in range(F_HEADS):
                kn = k_ref[rows, h * d:(h + 1) * d]
                c0, c1, c2 = _split3(g[:, 2 * M_HEADS + h:2 * M_HEADS + h + 1] * _LOG2E)
                extra = jnp.where(elane < 3, 1.0,
                                  jnp.where(elane == 3, -c0, jnp.where(elane == 4, -c1,
                                                                       jnp.where(elane == 5, -c2, 0.0))))
                kx_ref[rows, h * _FOX_KDIM:(h + 1) * _FOX_KDIM] = jnp.concatenate([kn, extra], axis=1).astype(BF16)
            return carry
        lax.fori_loop(0, lp // t, prep, 0)

    scale = d ** -0.5 * _LOG2E
    gq_rows = gcol_ref[pl.ds(pl.multiple_of(qi * t, t), t), :] * _LOG2E
    for h in range(F_HEADS):
        qn = q_ref[:, h * d:(h + 1) * d] * scale
        c0, c1, c2 = _split3(gq_rows[:, 2 * M_HEADS + h:2 * M_HEADS + h + 1])
        extra = jnp.where(elane == 0, c0, jnp.where(elane == 1, c1, jnp.where(elane == 2, c2,
                                                                              jnp.where(elane < 6, 1.0, 0.0))))
        qx_ref[h] = jnp.concatenate([qn, extra], axis=1).astype(BF16)

    acc_refs, m_refs, l_refs = (state_refs[i * F_HEADS:(i + 1) * F_HEADS] for i in range(3))
    for h in range(F_HEADS):
        m_refs[h][...] = jnp.full_like(m_refs[h], NEG_BIG)
        l_refs[h][...] = jnp.zeros_like(l_refs[h])
        acc_refs[h][...] = jnp.zeros_like(acc_refs[h])

    kk = lax.broadcasted_iota(jnp.int32, (t, t), 0)
    qq = lax.broadcasted_iota(jnp.int32, (t, t), 1)

    def step(jb, mask):
        start = pl.multiple_of(jb * t, t)
        scores = []
        for h in range(F_HEADS):
            kx = kx_ref[pl.ds(start, t), h * _FOX_KDIM:(h + 1) * _FOX_KDIM]
            scores.append(lax.dot_general(kx, qx_ref[h], _NT, preferred_element_type=F32))
        probs, alphas = [], []
        for h in range(F_HEADS):
            s = scores[h] if mask is None else jnp.where(mask, scores[h], NEG_BIG)
            m_old = m_refs[h][...]
            m_new = jnp.maximum(m_old, jnp.max(s, axis=0, keepdims=True))
            p = jnp.exp2(s - m_new)
            alpha = jnp.exp2(m_old - m_new)
            l_refs[h][...] = alpha * l_refs[h][...] + jnp.sum(p, axis=0, keepdims=True)
            m_refs[h][...] = m_new
            probs.append(p.astype(BF16))
            alphas.append(alpha)
        for h in range(F_HEADS):
            acc_refs[h][...] = alphas[h] * acc_refs[h][...] + _dot(vt_ref[h * d:(h + 1) * d, pl.ds(start, t)],
                                                                  probs[h])

    step(0, kk >= first_real)
    lax.fori_loop(1, qi, lambda jb, c: (step(jb, None), c)[1], 0)
    step(qi, kk <= qq)
    yt = jnp.concatenate([acc_refs[h][...] / l_refs[h][...] for h in range(F_HEADS)], axis=0)
    y_ref[...] = yt.T.astype(y_ref.dtype)


def _fox(zf, vt, gcol, first_real):
    b, lp, _ = zf.shape
    t = FOX_BLOCK
    nb = lp // t
    return pl.pallas_call(
        functools.partial(_fox_kernel, first_real),
        grid=(b, nb - 1),
        in_specs=[pl.BlockSpec((None, t, F_WIDTH), lambda i, j: (i, j + 1, 0)),
                  pl.BlockSpec((None, lp, F_WIDTH), lambda i, j: (i, 0, 1)),
                  pl.BlockSpec((F_WIDTH, lp), lambda i, j: (0, i)),
                  pl.BlockSpec((None, lp, GATE_LANES), lambda i, j: (i, 0, 0))],
        out_specs=pl.BlockSpec((None, t, F_WIDTH), lambda i, j: (i, j, 0)),
        out_shape=jax.ShapeDtypeStruct((b, lp - t, F_WIDTH), BF16),
        scratch_shapes=[pltpu.VMEM((lp, F_HEADS * _FOX_KDIM), BF16),
                        pltpu.VMEM((F_WIDTH, lp), BF16),
                        pltpu.VMEM((F_HEADS, t, _FOX_KDIM), BF16)]
                       + [pltpu.VMEM((F_HEAD_DIM, t), F32)] * F_HEADS
                       + [pltpu.VMEM((1, t), F32)] * F_HEADS
                       + [pltpu.VMEM((1, t), F32)] * F_HEADS,
        compiler_params=_params("parallel", "arbitrary"),
        name="fox",
    )(zf, zf, vt, gcol)


def _outproj_kernel(ym_ref, yf_ref, x_ref, wm_ref, wf_ref, g_ref, h2_ref, xn_ref):
    h2 = x_ref[...] + (_dot(ym_ref[...], wm_ref[...]) + _dot(yf_ref[...], wf_ref[...]))
    h2_ref[...] = h2
    xn = h2 * lax.rsqrt(jnp.mean(h2 * h2, axis=-1, keepdims=True) + EPS) * g_ref[...]
    xn_ref[...] = xn.astype(BF16)


def _outproj(ym, yf, x2d, wm, wf, g, tm):
    n, d = x2d.shape
    row = lambda w: pl.BlockSpec((tm, w), lambda i: (i, 0))
    full = lambda a: pl.BlockSpec(a.shape, lambda i: (0, 0))
    return pl.pallas_call(
        _outproj_kernel,
        grid=(n // tm,),
        in_specs=[row(ym.shape[1]), row(yf.shape[1]), row(d), full(wm), full(wf), full(g)],
        out_specs=[row(d), row(d)],
        out_shape=[jax.ShapeDtypeStruct((n, d), F32),
                   jax.ShapeDtypeStruct((n, d), BF16)],
        compiler_params=_params("parallel"),
        name="outproj",
    )(ym, yf, x2d, wm, wf, g)


_TOP = PEER_TOPK + 1
_TOP_ROWS = 24


def _merge_exchange_network(n):
    t = max(1, (n - 1).bit_length())
    pairs = []
    p = 1 << (t - 1)
    while p > 0:
        q, r, d = 1 << (t - 1), 0, p
        while d > 0:
            pairs += [(i, i + d) for i in range(n - d) if (i & p) == r]
            d, q, r = q - p, q >> 1, p
        p >>= 1
    return pairs


def _pop_largest(s):
    m = s.shape[0] // 8
    cols = [s[8 * i:8 * (i + 1), :] for i in range(m)]
    for i, j in _merge_exchange_network(m):
        cols[i], cols[j] = jnp.maximum(cols[i], cols[j]), jnp.minimum(cols[i], cols[j])
    values, counts = [], []
    for it in range(_TOP):
        best = jnp.max(cols[0], axis=0, keepdims=True)
        eq = cols[0] == best
        values.append(best)
        counts.append(jnp.sum(jnp.where(eq, 1.0, 0.0), axis=0, keepdims=True))
        live = min(m, _TOP - it)
        for d in range(live - 1):
            cols[d] = jnp.where(eq, cols[d + 1], cols[d])
        if live == m:
            cols[m - 1] = jnp.where(eq, -jnp.inf, cols[m - 1])
    return values, counts


def _top_values(s):
    tb = s.shape[1]
    rank = lax.broadcasted_iota(jnp.int32, (_TOP_ROWS, tb), 0).astype(F32)
    vals = None
    cnt = jnp.zeros((1, tb), F32)
    for best, k in zip(*_pop_largest(s)):
        vals = jnp.where(rank >= cnt, best, -jnp.inf if vals is None else vals)
        cnt = cnt + k
    return jnp.where(rank < _TOP, vals, -jnp.inf)


def _pair_threshold(a, b):
    k = PEER_TOPK
    slabs = [a[0:1, :] + b[0:_TOP_ROWS, :]]
    for r in range(1, 8):
        slabs.append(a[r:r + 1, :] + b[0:8, :])
    slabs.append(a[8:_TOP_ROWS, :] + b[0:1, :])
    top = a[0:1, :] + b[0:1, :]
    cnt = jnp.zeros_like(top)
    v_k = top
    v_k1 = top
    z = jnp.zeros_like(top)
    for best, n_eq in zip(*_pop_largest(jnp.concatenate(slabs, axis=0))):
        take = jnp.clip(k - cnt, 0.0, n_eq)
        z = z + take * jnp.exp(best - top)
        v_k = jnp.where(cnt < k, best, v_k)
        v_k1 = jnp.where(cnt < k + 1, best, v_k1)
        cnt = cnt + n_eq
    return 0.5 * (v_k + v_k1), top, z


def _count_leading(sorted_rows, pred):
    out = None
    for c in range(_TOP):
        hit = pred(sorted_rows[c:c + 1, :])
        out = jnp.where(hit, c + 1.0, 0.0 if out is None else out)
    return out


def _peer_route_kernel(xn_ref, wq_ref, kh_ref, kl_ref, r2_ref, e2_ref, n_ref, e1_ref):
    q = _dot(xn_ref[...], wq_ref[...])
    for h in range(PEER_HEADS):
        st = []
        for p in range(2):
            i = 2 * h + p
            qh, ql = _split_bf16(q[:, i * PEER_HALF:(i + 1) * PEER_HALF])
            st.append(_dot3(kh_ref[i], kl_ref[i], qh, ql, _NT))
        a = _top_values(st[0])
        b = _top_values(st[1])
        thr, top, z = _pair_threshold(a, b)
        need = thr - st[0]
        above2 = _count_leading(b, lambda row: row > st[1])
        n1 = _count_leading(b, lambda row: row >= need)
        r2_ref[h] = pltpu.bitcast(above2.astype(BF16), jnp.uint32)
        e2_ref[h] = pltpu.bitcast(jnp.exp(st[1] - b[0:1, :]).astype(BF16), jnp.uint32)
        n_ref[h] = n1
        e1_ref[h] = jnp.exp(st[0] - a[0:1, :]) * (0.5 / z)


def _peer_route(xn, wq, kh, kl, tb):
    n, d = xn.shape
    row = pl.BlockSpec((tb, d), lambda i: (i, 0))
    full2 = lambda a: pl.BlockSpec(a.shape, lambda i: (0, 0))
    full3 = lambda a: pl.BlockSpec(a.shape, lambda i: (0, 0, 0))
    out = lambda rows: pl.BlockSpec((PEER_HEADS, rows, tb), lambda i: (0, 0, i))
    shape = lambda rows, dt: jax.ShapeDtypeStruct((PEER_HEADS, rows, n), dt)
    return pl.pallas_call(
        _peer_route_kernel,
        grid=(n // tb,),
        in_specs=[row, full2(wq), full3(kh), full3(kl)],
        out_specs=[out(N_KEYS // 2), out(N_KEYS // 2), out(N_KEYS), out(N_KEYS)],
        out_shape=[shape(N_KEYS // 2, jnp.uint32), shape(N_KEYS // 2, jnp.uint32),
                   shape(N_KEYS, F32), shape(N_KEYS, F32)],
        compiler_params=_params("parallel"),
        name="peer_route",
    )(xn, wq, kh, kl)


def _twice_gelu(x):
    return x * (1.0 + lax.erf(x * (2.0 ** -0.5)))


_PACK = 16


_ROWS_PER_PIECE = 2
_PIECES_PER_CHUNK = 2


def _pack_rows_kernel(x_ref, o_ref):
    o_ref[...] = pltpu.bitcast(x_ref[...].astype(BF16), jnp.uint32)


def _pack_rows_t_kernel(x_ref, o_ref):
    o_ref[...] = pltpu.bitcast(x_ref[...].T.astype(BF16), jnp.uint32)


def _pack_bf16(x, transpose, rows):
    n, c = x.shape
    if transpose:
        kern, out_shape, out_spec = _pack_rows_t_kernel, (c // 2, n), pl.BlockSpec((c // 2, rows), lambda i: (0, i))
    else:
        kern, out_shape, out_spec = _pack_rows_kernel, (n // 2, c), pl.BlockSpec((rows // 2, c), lambda i: (i, 0))
    return pl.pallas_call(
        kern,
        grid=(n // rows,),
        in_specs=[pl.BlockSpec((rows, c), lambda i: (i, 0))],
        out_specs=out_spec,
        out_shape=jax.ShapeDtypeStruct(out_shape, jnp.uint32),
        compiler_params=_params("parallel"),
        name="pack_t" if transpose else "pack",
    )(x)


def _peer_expert_kernel(ib, xn_ref, u_ref, vt_ref, r2_ref, e2_ref, n_ref, e1_ref, h2_ref, o_ref,
                        acc_ref, w_ref, act_ref):
    e = pl.program_id(1)
    tb = xn_ref.shape[0]
    d = 2 * vt_ref.shape[0]
    piece = _ROWS_PER_PIECE * N_KEYS
    chunk = _PIECES_PER_CHUNK * piece
    n_chunks = ib * N_KEYS // chunk
    out_rows = d // _PIECES_PER_CHUNK

    @pl.when(e == 0)
    def _():
        acc_ref[...] = jnp.zeros_like(acc_ref)

    def act_piece(c, k):
        lo = c * chunk + k * piece
        act_ref[c % 2, k * piece:(k + 1) * piece, :] = lax.dot_general(
            pltpu.bitcast(u_ref[lo // 2:(lo + piece) // 2, :], BF16), xn_ref[...], _NT,
            preferred_element_type=F32)

    def out_piece(c, k):
        rows = slice(k * out_rows, (k + 1) * out_rows)
        vt = pltpu.bitcast(vt_ref[k * out_rows // 2:(k + 1) * out_rows // 2, c * chunk:(c + 1) * chunk], BF16)
        acc_ref[rows, :] += _dot(vt, w_ref[c * chunk:(c + 1) * chunk, :])

    def gate_piece(c, k):
        i0 = (c * chunk + k * piece) // N_KEYS
        for tc in range(tb // 128):
            cols = slice(tc * 128, (tc + 1) * 128)
            bcast = lambda ref, h, i: jnp.broadcast_to(ref[h, i:i + 1, cols], (_PACK, 128)).astype(BF16)
            gates = [[None] * (N_KEYS // _PACK) for _ in range(_ROWS_PER_PIECE)]
            for h in range(PEER_HEADS):
                n1 = [bcast(n_ref, h, i0 + r) for r in range(_ROWS_PER_PIECE)]
                e1 = [bcast(e1_ref, h, i0 + r) for r in range(_ROWS_PER_PIECE)]
                for g in range(N_KEYS // _PACK):
                    ks = slice(g * _PACK // 2, (g + 1) * _PACK // 2)
                    above2 = pltpu.bitcast(r2_ref[h, ks, cols], BF16)
                    e2 = pltpu.bitcast(e2_ref[h, ks, cols], BF16)
                    for r in range(_ROWS_PER_PIECE):
                        term = e1[r] * jnp.where(above2 < n1[r], e2, jnp.zeros_like(e2))
                        gates[r][g] = term if gates[r][g] is None else gates[r][g] + term
            for r in range(_ROWS_PER_PIECE):
                for g in range(N_KEYS // _PACK):
                    lo = k * piece + r * N_KEYS + g * _PACK
                    act = act_ref[c % 2, lo:lo + _PACK, cols]
                    w_ref[c * chunk + lo:c * chunk + lo + _PACK, cols] = gates[r][g] * _twice_gelu(act).astype(BF16)

    for k in range(_PIECES_PER_CHUNK):
        act_piece(0, k)
    for c in range(n_chunks):
        for k in range(_PIECES_PER_CHUNK):
            if c + 1 < n_chunks:
                act_piece(c + 1, k)
            if c > 0:
                out_piece(c - 1, k)
            gate_piece(c, k)
    for k in range(_PIECES_PER_CHUNK):
        out_piece(n_chunks - 1, k)

    @pl.when(e == pl.num_programs(1) - 1)
    def _():
        o_ref[...] = h2_ref[...] + acc_ref[...].T


def _peer_experts(xn, u_packed, vt_packed, r2, e2, n1, e1, h2, tb, ib):
    n, d = xn.shape
    ne = 2 * u_packed.shape[0]
    eb = ib * N_KEYS
    chunk = _PIECES_PER_CHUNK * _ROWS_PER_PIECE * N_KEYS
    assert eb % chunk == 0 and tb % 128 == 0
    gate_full = pl.BlockSpec((PEER_HEADS, N_KEYS // 2, tb), lambda t, e: (0, 0, t))
    gate_rows = pl.BlockSpec((PEER_HEADS, ib, tb), lambda t, e: (0, e, t))
    return pl.pallas_call(
        functools.partial(_peer_expert_kernel, ib),
        grid=(n // tb, ne // eb),
        in_specs=[pl.BlockSpec((tb, d), lambda t, e: (t, 0)),
                  pl.BlockSpec((eb // 2, d), lambda t, e: (e, 0)),
                  pl.BlockSpec((d // 2, eb), lambda t, e: (0, e)),
                  gate_full, gate_full, gate_rows, gate_rows,
                  pl.BlockSpec((tb, d), lambda t, e: (t, 0))],
        out_specs=pl.BlockSpec((tb, d), lambda t, e: (t, 0)),
        out_shape=jax.ShapeDtypeStruct((n, d), F32),
        scratch_shapes=[pltpu.VMEM((d, tb), F32),
                        pltpu.VMEM((eb, tb), BF16),
                        pltpu.VMEM((2, chunk, tb), F32)],
        compiler_params=_params("parallel", "arbitrary"),
        name="peer_experts",
    )(xn, u_packed, vt_packed, r2, e2, n1, e1, h2)


def _tile(n, prefer):
    for t in prefer:
        if n % t == 0:
            return t
    raise ValueError(f"no tile for {n}")


def kernel(x, meta_tokens, norm_mix, w_in, conv_qk, b_igate, b_fgate_m, m_out_norm, b_fgate_f,
           f_q_norm, f_k_norm, w_out, norm_ffn, peer_query, peer_sub_keys, peer_u, peer_v):
    assert w_in.shape[0] == 1, "single-layer block"
    b, seq, d = x.shape
    t = FOX_BLOCK
    assert seq % t == 0 and N_META <= SEQ_BLOCK and t % SEQ_BLOCK == 0
    first_real = t - N_META

    head = jnp.concatenate([jnp.zeros((first_real, d), x.dtype), meta_tokens.astype(x.dtype)], axis=0)
    w = w_in[0]
    o = 0
    cols = {}
    for name, size in (("mq", M_WIDTH), ("mk", M_WIDTH), ("mv", M_WIDTH), ("mo", M_WIDTH), ("mi", M_HEADS),
                       ("mf", M_HEADS), ("fq", F_WIDTH), ("fk", F_WIDTH), ("fv", F_WIDTH), ("ff", F_HEADS)):
        cols[name] = w[:, o:o + size]
        o += size
    w_m = jnp.concatenate([cols["mq"], cols["mk"], cols["mv"], cols["mo"]], axis=1).astype(BF16)
    w_f = jnp.concatenate([cols["fq"], cols["fk"]], axis=1).astype(BF16)
    w_vt = cols["fv"].T.astype(BF16)
    n_gate = 2 * M_HEADS + F_HEADS
    w_g = jnp.concatenate([cols["mi"], cols["mf"], cols["ff"], jnp.zeros((d, GATE_LANES - n_gate), F32)], axis=1)
    w_gh, w_gl = _split_bf16(w_g)
    gate_bias = jnp.concatenate([b_igate[0], b_fgate_m[0], b_fgate_f[0],
                                 jnp.zeros((GATE_LANES - n_gate,), F32)]).reshape(1, GATE_LANES)

    f_gain = jnp.concatenate([jnp.tile(f_q_norm[0], F_HEADS), jnp.tile(f_k_norm[0], F_HEADS)]).reshape(1, 2 * F_WIDTH)
    zm, zf, vt, zg = _inproj(x, head, norm_mix[0].reshape(1, d), w_m, w_f, f_gain, w_vt, w_gh, w_gl)
    gcol, grow = _gates(zg, gate_bias)
    y_m = _mlstm(zm, gcol, grow, conv_qk[0], m_out_norm[0].reshape(1, M_WIDTH), seq)
    y_f = _fox(zf, vt, gcol, first_real)
    n = b * seq
    wo = w_out[0].astype(BF16)
    h2, xh = _outproj(y_m.reshape(n, M_WIDTH), y_f.reshape(n, F_WIDTH), x.reshape(n, d),
                      wo[:M_WIDTH], wo[M_WIDTH:], norm_ffn[0].reshape(1, d), _tile(n, (512, 256, 128)))

    kh, kl = _split_bf16(peer_sub_keys[0].reshape(2 * PEER_HEADS, N_KEYS, PEER_HALF))
    r2, e2, n1, e1 = _peer_route(xh, peer_query[0].astype(BF16), kh, kl, _tile(n, (256, 128)))
    u_packed = _pack_bf16(peer_u[0], False, 1024)
    vt_packed = _pack_bf16(peer_v[0], True, 1024)
    out = _peer_experts(xh, u_packed, vt_packed, r2, e2, n1, e1, h2, _tile(n, (512, 256, 128)), 16)
    return out.reshape(b, seq, d)
```

```python
import functools

import jax
import jax.numpy as jnp
from jax import lax
from jax.experimental import pallas as pl
from jax.experimental.pallas import tpu as pltpu

F32 = jnp.float32
BF16 = jnp.bfloat16

EPS = 1e-6
N_META = 16
M_HEADS = 4
M_HEAD_DIM = 128
M_WIDTH = M_HEADS * M_HEAD_DIM
CONV_W = 4
F_HEADS = 8
F_HEAD_DIM = 64
F_WIDTH = F_HEADS * F_HEAD_DIM
PEER_HEADS = 8
N_KEYS = 128
PEER_TOPK = 16
PEER_HALF = 128

SEQ_BLOCK = 128
FOX_BLOCK = 256
GATE_LANES = 128
NEG_BIG = -1e30
VMEM_LIMIT = 56 * 1024 * 1024

_NT = (((1,), (1,)), ((), ()))
_TN = (((0,), (0,)), ((), ()))


def _params(*sem, flags=None):
    return pltpu.CompilerParams(dimension_semantics=sem, vmem_limit_bytes=VMEM_LIMIT, flags=flags)


def _split_bf16(a):
    hi = a.astype(BF16)
    lo = (a - hi.astype(F32)).astype(BF16)
    return hi, lo


def _dot(a, b):
    return jnp.dot(a, b, preferred_element_type=F32)


def _dot3(ah, al, bh, bl, dims=None):
    if dims is None:
        f = _dot
    else:
        f = lambda x, y: lax.dot_general(x, y, dims, preferred_element_type=F32)
    return f(ah, bh) + (f(ah, bl) + f(al, bh))


def _inproj_kernel(x_ref, head_ref, g_ref, wm_ref, wf_ref, fg_ref, wvt_ref, wgh_ref, wgl_ref,
                   zm_ref, zf_ref, vt_ref, zg_ref):
    h = jnp.where(pl.program_id(1) == 0, head_ref[...], x_ref[...])
    hn = h * lax.rsqrt(jnp.mean(h * h, axis=-1, keepdims=True) + EPS) * g_ref[...]
    hb, hl = _split_bf16(hn)
    zm_ref[...] = _dot(hb, wm_ref[...])
    zf = _dot(hb, wf_ref[...])
    low = lax.broadcasted_iota(jnp.int32, (zf.shape[0], 128), 1) < F_HEAD_DIM
    for c in range(zf.shape[1] // 128):
        x = zf[:, c * 128:(c + 1) * 128]
        x2 = x * x
        ms_lo = jnp.sum(jnp.where(low, x2, 0.0), axis=1, keepdims=True) * (1.0 / F_HEAD_DIM)
        ms_hi = jnp.sum(jnp.where(low, 0.0, x2), axis=1, keepdims=True) * (1.0 / F_HEAD_DIM)
        r = jnp.where(low, lax.rsqrt(ms_lo + EPS), lax.rsqrt(ms_hi + EPS))
        zf_ref[:, c * 128:(c + 1) * 128] = x * r * fg_ref[:, c * 128:(c + 1) * 128]
    vt_ref[...] = lax.dot_general(wvt_ref[...], hb, _NT, preferred_element_type=F32)
    zg_ref[...] = _dot3(hb, hl, wgh_ref[...], wgl_ref[...])


def _inproj(x, head, g, wm, wf, fg, wvt, wgh, wgl):
    b, seq, d = x.shape
    tm = head.shape[0]
    nb = seq // tm + 1
    lp = nb * tm
    full = lambda a: pl.BlockSpec(a.shape, lambda i, j: (0, 0))
    rows = lambda w: pl.BlockSpec((None, tm, w), lambda i, j: (i, j, 0))
    return pl.pallas_call(
        _inproj_kernel,
        grid=(b, nb),
        in_specs=[pl.BlockSpec((None, tm, d), lambda i, j: (i, jnp.maximum(j - 1, 0), 0)), full(head), full(g),
                  full(wm), full(wf), full(fg), full(wvt), full(wgh), full(wgl)],
        out_specs=[rows(wm.shape[1]), rows(wf.shape[1]),
                   pl.BlockSpec((wvt.shape[0], tm), lambda i, j: (0, i * nb + j)),
                   rows(GATE_LANES)],
        out_shape=[jax.ShapeDtypeStruct((b, lp, wm.shape[1]), F32),
                   jax.ShapeDtypeStruct((b, lp, wf.shape[1]), F32),
                   jax.ShapeDtypeStruct((wvt.shape[0], b * lp), F32),
                   jax.ShapeDtypeStruct((b, lp, GATE_LANES), F32)],
        compiler_params=_params("parallel", "arbitrary"),
        name="inproj",
    )(x, head, g, wm, wf, fg, wvt, wgh, wgl)


def _log_sigmoid(x):
    return jnp.minimum(x, 0.0) - jnp.log1p(jnp.exp(-jnp.abs(x)))


def _gates_kernel(zg_ref, bias_ref, gcol_ref, grow_ref):
    lp = zg_ref.shape[0]
    t = SEQ_BLOCK
    r = lax.broadcasted_iota(jnp.int32, (t, t), 0)
    c = lax.broadcasted_iota(jnp.int32, (t, t), 1)
    tri = jnp.where(r >= c, 1.0, 0.0).astype(BF16)
    lane = lax.broadcasted_iota(jnp.int32, (t, GATE_LANES), 1)
    carry = jnp.zeros((1, GATE_LANES), F32)
    for i in range(lp // t):
        pre = zg_ref[i * t:(i + 1) * t, :] + bias_ref[...]
        ls = _log_sigmoid(pre)
        l0 = ls.astype(BF16)
        r1 = ls - l0.astype(F32)
        l1 = r1.astype(BF16)
        l2 = (r1 - l1.astype(F32)).astype(BF16)
        cs = (_dot(tri, l0) + (_dot(tri, l1) + _dot(tri, l2))) + carry
        carry = cs[t - 1:t, :]
        out = jnp.where(lane < M_HEADS, pre, cs)
        gcol_ref[i * t:(i + 1) * t, :] = out
        grow_ref[:, i * t:(i + 1) * t] = out.T[:16, :]


def _gates(zg, bias):
    b, lp, _ = zg.shape
    return pl.pallas_call(
        _gates_kernel,
        grid=(b,),
        in_specs=[pl.BlockSpec((None, lp, GATE_LANES), lambda i: (i, 0, 0)),
                  pl.BlockSpec((1, GATE_LANES), lambda i: (0, 0))],
        out_specs=[pl.BlockSpec((None, lp, GATE_LANES), lambda i: (i, 0, 0)),
                   pl.BlockSpec((None, 16, lp), lambda i: (i, 0, 0))],
        out_shape=[jax.ShapeDtypeStruct((b, lp, GATE_LANES), F32),
                   jax.ShapeDtypeStruct((b, 16, lp), F32)],
        compiler_params=_params("parallel"),
        name="gates",
    )(zg, bias)


def _mlstm_kernel(has_prev, zm_ref, gcol_ref, grow_ref, gprev_ref, convw_ref, gain_ref, y_ref,
                  prevx_ref, ct_ref, n_ref, m_ref, pc_ref):
    j = pl.program_id(1)
    t = SEQ_BLOCK
    w2 = 2 * M_WIDTH

    @pl.when(j == 0)
    def _():
        prevx_ref[...] = jnp.zeros_like(prevx_ref)
        ct_ref[...] = jnp.zeros_like(ct_ref)
        n_ref[...] = jnp.zeros_like(n_ref)
        m_ref[...] = jnp.zeros_like(m_ref)
        pc_ref[...] = (jnp.broadcast_to(gprev_ref[7:8, :], pc_ref.shape) if has_prev
                       else jnp.zeros_like(pc_ref))

    xqk = zm_ref[:, :w2]
    prev = prevx_ref[...]
    rowi = lax.broadcasted_iota(jnp.int32, (t, w2), 0)
    acc = xqk * convw_ref[CONV_W - 1:CONV_W, :]
    for s in range(1, CONV_W):
        shifted = jnp.where(rowi < s, pltpu.roll(prev, s, 0), pltpu.roll(xqk, s, 0))
        acc = acc + shifted * convw_ref[CONV_W - 1 - s:CONV_W - s, :]
    prevx_ref[...] = xqk
    qk = acc * jax.nn.sigmoid(acc)

    gcol = gcol_ref[...]
    grow = grow_ref[...]
    rr = lax.broadcasted_iota(jnp.int32, (t, t), 0)
    cc = lax.broadcasted_iota(jnp.int32, (t, t), 1)
    causal = rr >= cc
    scale = M_HEAD_DIM ** -0.5

    heads = range(M_HEADS)
    hsl = [slice(h * M_HEAD_DIM, (h + 1) * M_HEAD_DIM) for h in heads]
    q = [qk[:, hsl[h]] for h in heads]
    k = [qk[:, M_WIDTH + h * M_HEAD_DIM:M_WIDTH + (h + 1) * M_HEAD_DIM] * scale for h in heads]
    qb = [x.astype(BF16) for x in q]
    kb = [x.astype(BF16) for x in k]
    vb = [zm_ref[:, w2 + h * M_HEAD_DIM:w2 + (h + 1) * M_HEAD_DIM].astype(BF16) for h in heads]
    ct = [ct_ref[h] for h in heads]
    n_row = [n_ref[h, 0:1, :] for h in heads]
    m_prev = [m_ref[h, 0:1, 0:1] for h in heads]

    qk_dot = [lax.dot_general(qb[h], kb[h], _NT, preferred_element_type=F32) for h in heads]
    q_ct = [_dot(qb[h], ct[h].astype(BF16)) for h in heads]

    li_c = [gcol[:, h:h + 1] for h in heads]
    li_r = [grow[h:h + 1, :] for h in heads]
    prev_cum = [pc_ref[0:1, M_HEADS + h:M_HEADS + h + 1] for h in heads]
    b_c = [gcol[:, M_HEADS + h:M_HEADS + h + 1] - prev_cum[h] for h in heads]
    b_r = [grow[M_HEADS + h:M_HEADS + h + 1, :] - prev_cum[h] for h in heads]
    dm = [jnp.where(causal, (b_c[h] - b_r[h]) + li_r[h], -jnp.inf) for h in heads]
    m_inter = [b_c[h] + m_prev[h] for h in heads]
    m_t = [jnp.maximum(m_inter[h], jnp.max(dm[h], axis=1, keepdims=True)) for h in heads]
    w_inter = [jnp.exp(m_inter[h] - m_t[h]) for h in heads]
    s_mat = [qk_dot[h] * jnp.exp(dm[h] - m_t[h]) for h in heads]

    num = [_dot(s_mat[h].astype(BF16), vb[h]) + w_inter[h] * q_ct[h] for h in heads]
    den = [jnp.sum(s_mat[h], axis=1, keepdims=True)
           + w_inter[h] * jnp.sum(q[h] * n_row[h], axis=1, keepdims=True) for h in heads]
    hh = [num[h] / jnp.maximum(jnp.abs(den[h]), jnp.exp(-m_t[h])) for h in heads]

    b_end = [b_c[h][t - 1:t, :] for h in heads]
    g_c = [(b_end[h] - b_c[h]) + li_c[h] for h in heads]
    g_r = [(b_end[h] - b_r[h]) + li_r[h] for h in heads]
    m_new = [jnp.maximum(b_end[h] + m_prev[h], jnp.max(g_r[h], axis=1, keepdims=True)) for h in heads]
    decay = [jnp.exp(b_end[h] + m_prev[h] - m_new[h]) for h in heads]
    kw = [k[h] * jnp.exp(g_c[h] - m_new[h]) for h in heads]
    for h in heads:
        ct_ref[h] = decay[h] * ct[h] + lax.dot_general(kw[h].astype(BF16), vb[h], _TN, preferred_element_type=F32)
        n_ref[h] = jnp.broadcast_to(decay[h] * n_row[h] + jnp.sum(kw[h], axis=0, keepdims=True), (8, M_HEAD_DIM))
        m_ref[h] = jnp.broadcast_to(m_new[h], (8, 128))

    for h in heads:
        o_pre = zm_ref[:, w2 + M_WIDTH + h * M_HEAD_DIM:w2 + M_WIDTH + (h + 1) * M_HEAD_DIM]
        hn = hh[h] * lax.rsqrt(jnp.mean(hh[h] * hh[h], axis=-1, keepdims=True) + EPS) * gain_ref[:, hsl[h]]
        y_ref[:, hsl[h]] = (hn * jax.nn.sigmoid(o_pre)).astype(y_ref.dtype)

    pc_ref[...] = jnp.broadcast_to(gcol[t - 1:t, :], (8, GATE_LANES))


def _mlstm(zm, gcol, grow, convw, gain, seq):
    b, lp, wz = zm.shape
    t = SEQ_BLOCK
    nb = seq // t + 1
    skip = lp // t - nb
    prev_rows = max(skip * t // 8 - 1, 0)
    return pl.pallas_call(
        functools.partial(_mlstm_kernel, skip > 0),
        grid=(b, nb),
        in_specs=[pl.BlockSpec((None, t, wz), lambda i, j: (i, j + skip, 0)),
                  pl.BlockSpec((None, t, GATE_LANES), lambda i, j: (i, j + skip, 0)),
                  pl.BlockSpec((None, 16, t), lambda i, j: (i, 0, j + skip)),
                  pl.BlockSpec((None, 8, GATE_LANES), lambda i, j: (i, prev_rows, 0)),
                  pl.BlockSpec(convw.shape, lambda i, j: (0, 0)),
                  pl.BlockSpec(gain.shape, lambda i, j: (0, 0))],
        out_specs=pl.BlockSpec((None, t, M_WIDTH), lambda i, j: (i, jnp.maximum(j - 1, 0), 0)),
        out_shape=jax.ShapeDtypeStruct((b, seq, M_WIDTH), BF16),
        scratch_shapes=[pltpu.VMEM((t, 2 * M_WIDTH), F32),
                        pltpu.VMEM((M_HEADS, M_HEAD_DIM, M_HEAD_DIM), F32),
                        pltpu.VMEM((M_HEADS, 8, M_HEAD_DIM), F32),
                        pltpu.VMEM((M_HEADS, 8, 128), F32),
                        pltpu.VMEM((8, GATE_LANES), F32)],
        compiler_params=_params("parallel", "arbitrary"),
        name="mlstm",
    )(zm, gcol, grow, gcol, convw, gain)


_FOX_KDIM = 128
_LOG2E = 1.4426950408889634


def _split3(c):
    c0 = c.astype(BF16).astype(F32)
    r1 = c - c0
    c1 = r1.astype(BF16).astype(F32)
    c2 = (r1 - c1).astype(BF16).astype(F32)
    return c0, c1, c2


def _fox_kernel(first_real, q_ref, k_ref, v_ref, gcol_ref, y_ref,
                kx_ref, vt_ref, qx_ref, *state_refs):
    qi = pl.program_id(1) + 1
    t = FOX_BLOCK
    d = F_HEAD_DIM
    lp = k_ref.shape[0]
    elane = lax.broadcasted_iota(jnp.int32, (t, _FOX_KDIM - d), 1)

    @pl.when(qi == 1)
    def _():
        def prep(c, carry):
            r0 = pl.multiple_of(c * t, t)
            rows = pl.ds(r0, t)
            vt_ref[:, rows] = v_ref[:, rows].astype(BF16)
            g = gcol_ref[rows, :]
            for h in range(F_HEADS):
                kn = k_ref[rows, h * d:(h + 1) * d]
                c0, c1, c2 = _split3(g[:, 2 * M_HEADS + h:2 * M_HEADS + h + 1] * _LOG2E)
                extra = jnp.where(elane < 3, 1.0,
                                  jnp.where(elane == 3, -c0, jnp.where(elane == 4, -c1,
                                                                       jnp.where(elane == 5, -c2, 0.0))))
                kx_ref[rows, h * _FOX_KDIM:(h + 1) * _FOX_KDIM] = jnp.concatenate([kn, extra], axis=1).astype(BF16)
            return carry
        lax.fori_loop(0, lp // t, prep, 0)

    scale = d ** -0.5 * _LOG2E
    gq_rows = gcol_ref[pl.ds(pl.multiple_of(qi * t, t), t), :] * _LOG2E
    for h in range(F_HEADS):
        qn = q_ref[:, h * d:(h + 1) * d] * scale
        c0, c1, c2 = _split3(gq_rows[:, 2 * M_HEADS + h:2 * M_HEADS + h + 1])
        extra = jnp.where(elane == 0, c0, jnp.where(elane == 1, c1, jnp.where(elane == 2, c2,
                                                                              jnp.where(elane < 6, 1.0, 0.0))))
        qx_ref[h] = jnp.concatenate([qn, extra], axis=1).astype(BF16)

    acc_refs, m_refs, l_refs = (state_refs[i * F_HEADS:(i + 1) * F_HEADS] for i in range(3))
    for h in range(F_HEADS):
        m_refs[h][...] = jnp.full_like(m_refs[h], NEG_BIG)
        l_refs[h][...] = jnp.zeros_like(l_refs[h])
        acc_refs[h][...] = jnp.zeros_like(acc_refs[h])

    kk = lax.broadcasted_iota(jnp.int32, (t, t), 0)
    qq = lax.broadcasted_iota(jnp.int32, (t, t), 1)

    def step(jb, mask):
        start = pl.multiple_of(jb * t, t)
        scores = []
        for h in range(F_HEADS):
            kx = kx_ref[pl.ds(start, t), h * _FOX_KDIM:(h + 1) * _FOX_KDIM]
            scores.append(lax.dot_general(kx, qx_ref[h], _NT, preferred_element_type=F32))
        probs, alphas = [], []
        for h in range(F_HEADS):
            s = scores[h] if mask is None else jnp.where(mask, scores[h], NEG_BIG)
            m_old = m_refs[h][...]
            m_new = jnp.maximum(m_old, jnp.max(s, axis=0, keepdims=True))
            p = jnp.exp2(s - m_new)
            alpha = jnp.exp2(m_old - m_new)
            l_refs[h][...] = alpha * l_refs[h][...] + jnp.sum(p, axis=0, keepdims=True)
            m_refs[h][...] = m_new
            probs.append(p.astype(BF16))
            alphas.append(alpha)
        for h in range(F_HEADS):
            acc_refs[h][...] = alphas[h] * acc_refs[h][...] + _dot(vt_ref[h * d:(h + 1) * d, pl.ds(start, t)],
                                                                  probs[h])

    step(0, kk >= first_real)
    lax.fori_loop(1, qi, lambda jb, c: (step(jb, None), c)[1], 0)
    step(qi, kk <= qq)
    yt = jnp.concatenate([acc_refs[h][...] / l_refs[h][...] for h in range(F_HEADS)], axis=0)
    y_ref[...] = yt.T.astype(y_ref.dtype)


def _fox(zf, vt, gcol, first_real):
    b, lp, _ = zf.shape
    t = FOX_BLOCK
    nb = lp // t
    return pl.pallas_call(
        functools.partial(_fox_kernel, first_real),
        grid=(b, nb - 1),
        in_specs=[pl.BlockSpec((None, t, F_WIDTH), lambda i, j: (i, j + 1, 0)),
                  pl.BlockSpec((None, lp, F_WIDTH), lambda i, j: (i, 0, 1)),
                  pl.BlockSpec((F_WIDTH, lp), lambda i, j: (0, i)),
                  pl.BlockSpec((None, lp, GATE_LANES), lambda i, j: (i, 0, 0))],
        out_specs=pl.BlockSpec((None, t, F_WIDTH), lambda i, j: (i, j, 0)),
        out_shape=jax.ShapeDtypeStruct((b, lp - t, F_WIDTH), BF16),
        scratch_shapes=[pltpu.VMEM((lp, F_HEADS * _FOX_KDIM), BF16),
                        pltpu.VMEM((F_WIDTH, lp), BF16),
                        pltpu.VMEM((F_HEADS, t, _FOX_KDIM), BF16)]
                       + [pltpu.VMEM((F_HEAD_DIM, t), F32)] * F_HEADS
                       + [pltpu.VMEM((1, t), F32)] * F_HEADS
                       + [pltpu.VMEM((1, t), F32)] * F_HEADS,
        compiler_params=_params("parallel", "arbitrary"),
        name="fox",
    )(zf, zf, vt, gcol)


def _outproj_kernel(ym_ref, yf_ref, x_ref, wm_ref, wf_ref, g_ref, h2_ref, xn_ref):
    h2 = x_ref[...] + (_dot(ym_ref[...], wm_ref[...]) + _dot(yf_ref[...], wf_ref[...]))
    h2_ref[...] = h2
    xn = h2 * lax.rsqrt(jnp.mean(h2 * h2, axis=-1, keepdims=True) + EPS) * g_ref[...]
    xn_ref[...] = xn.astype(BF16)


def _outproj(ym, yf, x2d, wm, wf, g, tm):
    n, d = x2d.shape
    row = lambda w: pl.BlockSpec((tm, w), lambda i: (i, 0))
    full = lambda a: pl.BlockSpec(a.shape, lambda i: (0, 0))
    return pl.pallas_call(
        _outproj_kernel,
        grid=(n // tm,),
        in_specs=[row(ym.shape[1]), row(yf.shape[1]), row(d), full(wm), full(wf), full(g)],
        out_specs=[row(d), row(d)],
        out_shape=[jax.ShapeDtypeStruct((n, d), F32),
                   jax.ShapeDtypeStruct((n, d), BF16)],
        compiler_params=_params("parallel"),
        name="outproj",
    )(ym, yf, x2d, wm, wf, g)


_TOP = PEER_TOPK + 1
_TOP_ROWS = 24


def _merge_exchange_network(n):
    t = max(1, (n - 1).bit_length())
    pairs = []
    p = 1 << (t - 1)
    while p > 0:
        q, r, d = 1 << (t - 1), 0, p
        while d > 0:
            pairs += [(i, i + d) for i in range(n - d) if (i & p) == r]
            d, q, r = q - p, q >> 1, p
        p >>= 1
    return pairs


def _pop_largest(s):
    m = s.shape[0] // 8
    cols = [s[8 * i:8 * (i + 1), :] for i in range(m)]
    for i, j in _merge_exchange_network(m):
        cols[i], cols[j] = jnp.maximum(cols[i], cols[j]), jnp.minimum(cols[i], cols[j])
    values, counts = [], []
    for it in range(_TOP):
        best = jnp.max(cols[0], axis=0, keepdims=True)
        eq = cols[0] == best
        values.append(best)
        counts.append(jnp.sum(jnp.where(eq, 1.0, 0.0), axis=0, keepdims=True))
        live = min(m, _TOP - it)
        for d in range(live - 1):
            cols[d] = jnp.where(eq, cols[d + 1], cols[d])
        if live == m:
            cols[m - 1] = jnp.where(eq, -jnp.inf, cols[m - 1])
    return values, counts


def _top_values(s):
    tb = s.shape[1]
    rank = lax.broadcasted_iota(jnp.int32, (_TOP_ROWS, tb), 0).astype(F32)
    vals = None
    cnt = jnp.zeros((1, tb), F32)
    for best, k in zip(*_pop_largest(s)):
        vals = jnp.where(rank >= cnt, best, -jnp.inf if vals is None else vals)
        cnt = cnt + k
    return jnp.where(rank < _TOP, vals, -jnp.inf)


def _pair_threshold(a, b):
    k = PEER_TOPK
    slabs = [a[0:1, :] + b[0:_TOP_ROWS, :]]
    for r in range(1, 8):
        slabs.append(a[r:r + 1, :] + b[0:8, :])
    slabs.append(a[8:_TOP_ROWS, :] + b[0:1, :])
    top = a[0:1, :] + b[0:1, :]
    cnt = jnp.zeros_like(top)
    v_k = top
    v_k1 = top
    z = jnp.zeros_like(top)
    for best, n_eq in zip(*_pop_largest(jnp.concatenate(slabs, axis=0))):
        take = jnp.clip(k - cnt, 0.0, n_eq)
        z = z + take * jnp.exp(best - top)
        v_k = jnp.where(cnt < k, best, v_k)
        v_k1 = jnp.where(cnt < k + 1, best, v_k1)
        cnt = cnt + n_eq
    return 0.5 * (v_k + v_k1), top, z


def _count_leading(sorted_rows, pred):
    out = None
    for c in range(_TOP):
        hit = pred(sorted_rows[c:c + 1, :])
        out = jnp.where(hit, c + 1.0, 0.0 if out is None else out)
    return out


def _peer_route_kernel(xn_ref, wq_ref, kh_ref, kl_ref, r2_ref, e2_ref, n_ref, e1_ref):
    q = _dot(xn_ref[...], wq_ref[...])
    for h in range(PEER_HEADS):
        st = []
        for p in range(2):
            i = 2 * h + p
            qh, ql = _split_bf16(q[:, i * PEER_HALF:(i + 1) * PEER_HALF])
            st.append(_dot3(kh_ref[i], kl_ref[i], qh, ql, _NT))
        a = _top_values(st[0])
        b = _top_values(st[1])
        thr, top, z = _pair_threshold(a, b)
        need = thr - st[0]
        above2 = _count_leading(b, lambda row: row > st[1])
        n1 = _count_leading(b, lambda row: row >= need)
        r2_ref[h] = pltpu.bitcast(above2.astype(BF16), jnp.uint32)
        e2_ref[h] = pltpu.bitcast(jnp.exp(st[1] - b[0:1, :]).astype(BF16), jnp.uint32)
        n_ref[h] = n1
        e1_ref[h] = jnp.exp(st[0] - a[0:1, :]) * (0.5 / z)


def _peer_route(xn, wq, kh, kl, tb):
    n, d = xn.shape
    row = pl.BlockSpec((tb, d), lambda i: (i, 0))
    full2 = lambda a: pl.BlockSpec(a.shape, lambda i: (0, 0))
    full3 = lambda a: pl.BlockSpec(a.shape, lambda i: (0, 0, 0))
    out = lambda rows: pl.BlockSpec((PEER_HEADS, rows, tb), lambda i: (0, 0, i))
    shape = lambda rows, dt: jax.ShapeDtypeStruct((PEER_HEADS, rows, n), dt)
    return pl.pallas_call(
        _peer_route_kernel,
        grid=(n // tb,),
        in_specs=[row, full2(wq), full3(kh), full3(kl)],
        out_specs=[out(N_KEYS // 2), out(N_KEYS // 2), out(N_KEYS), out(N_KEYS)],
        out_shape=[shape(N_KEYS // 2, jnp.uint32), shape(N_KEYS // 2, jnp.uint32),
                   shape(N_KEYS, F32), shape(N_KEYS, F32)],
        compiler_params=_params("parallel"),
        name="peer_route",
    )(xn, wq, kh, kl)


def _twice_gelu(x):
    return x * (1.0 + lax.erf(x * (2.0 ** -0.5)))


_PACK = 16


_ROWS_PER_PIECE = 2
_OUT_PIECES = 2


def _pack_rows_kernel(x_ref, o_ref):
    o_ref[...] = pltpu.bitcast(x_ref[...].astype(BF16), jnp.uint32)


def _pack_rows_t_kernel(x_ref, o_ref):
    o_ref[...] = pltpu.bitcast(x_ref[...].T.astype(BF16), jnp.uint32)


def _pack_bf16(x, transpose, rows):
    n, c = x.shape
    if transpose:
        kern, out_shape, out_spec = _pack_rows_t_kernel, (c // 2, n), pl.BlockSpec((c // 2, rows), lambda i: (0, i))
    else:
        kern, out_shape, out_spec = _pack_rows_kernel, (n // 2, c), pl.BlockSpec((rows // 2, c), lambda i: (i, 0))
    return pl.pallas_call(
        kern,
        grid=(n // rows,),
        in_specs=[pl.BlockSpec((rows, c), lambda i: (i, 0))],
        out_specs=out_spec,
        out_shape=jax.ShapeDtypeStruct(out_shape, jnp.uint32),
        compiler_params=_params("parallel"),
        name="pack_t" if transpose else "pack",
    )(x)


def _peer_expert_kernel(ib, n_e, xn_ref, u_ref, vt_ref, r2_ref, e2_ref, n_ref, e1_ref, h2_ref, o_ref,
                        acc_ref, w0_ref, w1_ref, act0_ref, act1_ref):
    s = pl.program_id(0)
    tb = xn_ref.shape[0]
    d = 2 * vt_ref.shape[0]
    eb = ib * N_KEYS
    piece = _ROWS_PER_PIECE * N_KEYS
    n_pieces = eb // piece
    out_rows = d // _OUT_PIECES
    lagged = (s + n_e - 2) % n_e

    @pl.when(s == 0)
    def _():
        for ref in (w0_ref, w1_ref, act0_ref, act1_ref):
            ref[...] = jnp.zeros_like(ref)

    @pl.when((s == 0) | ((s >= 2) & (lagged == 0)))
    def _():
        acc_ref[...] = jnp.zeros_like(acc_ref)

    def body(act_w, act_r, w_w, w_r):
        def act_piece(k):
            lo = k * piece
            act_w[lo:lo + piece, :] = lax.dot_general(
                pltpu.bitcast(u_ref[lo // 2:(lo + piece) // 2, :], BF16), xn_ref[...], _NT,
                preferred_element_type=F32)

        def out_piece(k):
            r, c = k % _OUT_PIECES, k // _OUT_PIECES
            chunk = eb // (n_pieces // _OUT_PIECES)
            rows = slice(r * out_rows, (r + 1) * out_rows)
            vt = pltpu.bitcast(vt_ref[r * out_rows // 2:(r + 1) * out_rows // 2, c * chunk:(c + 1) * chunk], BF16)
            acc_ref[rows, :] += _dot(vt, w_r[c * chunk:(c + 1) * chunk, :])

        def gate_piece(k):
            i0 = k * _ROWS_PER_PIECE
            for tc in range(tb // 128):
                cols = slice(tc * 128, (tc + 1) * 128)
                bcast = lambda ref, h, i: jnp.broadcast_to(ref[h, i:i + 1, cols], (_PACK, 128)).astype(BF16)
                gates = [[None] * (N_KEYS // _PACK) for _ in range(_ROWS_PER_PIECE)]
                for h in range(PEER_HEADS):
                    n1 = [bcast(n_ref, h, i0 + r) for r in range(_ROWS_PER_PIECE)]
                    e1 = [bcast(e1_ref, h, i0 + r) for r in range(_ROWS_PER_PIECE)]
                    for g in range(N_KEYS // _PACK):
                        ks = slice(g * _PACK // 2, (g + 1) * _PACK // 2)
                        above2 = pltpu.bitcast(r2_ref[h, ks, cols], BF16)
                        e2 = pltpu.bitcast(e2_ref[h, ks, cols], BF16)
                        for r in range(_ROWS_PER_PIECE):
                            term = e1[r] * jnp.where(above2 < n1[r], e2, jnp.zeros_like(e2))
                            gates[r][g] = term if gates[r][g] is None else gates[r][g] + term
                for r in range(_ROWS_PER_PIECE):
                    for g in range(N_KEYS // _PACK):
                        lo = k * piece + r * N_KEYS + g * _PACK
                        w_w[lo:lo + _PACK, cols] = gates[r][g] * _twice_gelu(act_r[lo:lo + _PACK, cols]).astype(BF16)

        for k in range(n_pieces):
            act_piece(k)
            out_piece(k)
            gate_piece(k)

    @pl.when(s % 2 == 0)
    def _():
        body(act0_ref, act1_ref, w1_ref, w0_ref)

    @pl.when(s % 2 == 1)
    def _():
        body(act1_ref, act0_ref, w0_ref, w1_ref)

    @pl.when((s >= 2) & (lagged == n_e - 1))
    def _():
        o_ref[...] = h2_ref[...] + acc_ref[...].T


def _peer_experts(xn, u_packed, vt_packed, r2, e2, n1, e1, h2, tb, ib):
    n, d = xn.shape
    ne = 2 * u_packed.shape[0]
    eb = ib * N_KEYS
    n_e, n_t = ne // eb, n // tb
    last = n_t * n_e - 1
    assert eb % (_ROWS_PER_PIECE * N_KEYS * _OUT_PIECES) == 0 and tb % 128 == 0

    def pair(lag):
        def f(s):
            p = jnp.clip(s - lag, 0, last)
            return p // n_e, p % n_e
        return f

    cur, prev, prev2 = pair(0), pair(1), pair(2)
    gate_full = pl.BlockSpec((PEER_HEADS, N_KEYS // 2, tb), lambda s: (0, 0, prev(s)[0]))
    gate_rows = pl.BlockSpec((PEER_HEADS, ib, tb), lambda s: (0, prev(s)[1], prev(s)[0]))
    return pl.pallas_call(
        functools.partial(_peer_expert_kernel, ib, n_e),
        grid=(n_t * n_e + 2,),
        in_specs=[pl.BlockSpec((tb, d), lambda s: (cur(s)[0], 0)),
                  pl.BlockSpec((eb // 2, d), lambda s: (cur(s)[1], 0)),
                  pl.BlockSpec((d // 2, eb), lambda s: (0, prev2(s)[1])),
                  gate_full, gate_full, gate_rows, gate_rows,
                  pl.BlockSpec((tb, d), lambda s: (prev2(s)[0], 0))],
        out_specs=pl.BlockSpec((tb, d), lambda s: (prev2(s)[0], 0)),
        out_shape=jax.ShapeDtypeStruct((n, d), F32),
        scratch_shapes=[pltpu.VMEM((d, tb), F32),
                        pltpu.VMEM((eb, tb), BF16), pltpu.VMEM((eb, tb), BF16),
                        pltpu.VMEM((eb, tb), F32), pltpu.VMEM((eb, tb), F32)],
        compiler_params=_params("arbitrary"),
        name="peer_experts",
    )(xn, u_packed, vt_packed, r2, e2, n1, e1, h2)


def _tile(n, prefer):
    for t in prefer:
        if n % t == 0:
            return t
    raise ValueError(f"no tile for {n}")


def kernel(x, meta_tokens, norm_mix, w_in, conv_qk, b_igate, b_fgate_m, m_out_norm, b_fgate_f,
           f_q_norm, f_k_norm, w_out, norm_ffn, peer_query, peer_sub_keys, peer_u, peer_v):
    assert w_in.shape[0] == 1, "single-layer block"
    b, seq, d = x.shape
    t = FOX_BLOCK
    assert seq % t == 0 and N_META <= SEQ_BLOCK and t % SEQ_BLOCK == 0
    first_real = t - N_META

    head = jnp.concatenate([jnp.zeros((first_real, d), x.dtype), meta_tokens.astype(x.dtype)], axis=0)
    w = w_in[0]
    o = 0
    cols = {}
    for name, size in (("mq", M_WIDTH), ("mk", M_WIDTH), ("mv", M_WIDTH), ("mo", M_WIDTH), ("mi", M_HEADS),
                       ("mf", M_HEADS), ("fq", F_WIDTH), ("fk", F_WIDTH), ("fv", F_WIDTH), ("ff", F_HEADS)):
        cols[name] = w[:, o:o + size]
        o += size
    w_m = jnp.concatenate([cols["mq"], cols["mk"], cols["mv"], cols["mo"]], axis=1).astype(BF16)
    w_f = jnp.concatenate([cols["fq"], cols["fk"]], axis=1).astype(BF16)
    w_vt = cols["fv"].T.astype(BF16)
    n_gate = 2 * M_HEADS + F_HEADS
    w_g = jnp.concatenate([cols["mi"], cols["mf"], cols["ff"], jnp.zeros((d, GATE_LANES - n_gate), F32)], axis=1)
    w_gh, w_gl = _split_bf16(w_g)
    gate_bias = jnp.concatenate([b_igate[0], b_fgate_m[0], b_fgate_f[0],
                                 jnp.zeros((GATE_LANES - n_gate,), F32)]).reshape(1, GATE_LANES)

    f_gain = jnp.concatenate([jnp.tile(f_q_norm[0], F_HEADS), jnp.tile(f_k_norm[0], F_HEADS)]).reshape(1, 2 * F_WIDTH)
    zm, zf, vt, zg = _inproj(x, head, norm_mix[0].reshape(1, d), w_m, w_f, f_gain, w_vt, w_gh, w_gl)
    gcol, grow = _gates(zg, gate_bias)
    y_m = _mlstm(zm, gcol, grow, conv_qk[0], m_out_norm[0].reshape(1, M_WIDTH), seq)
    y_f = _fox(zf, vt, gcol, first_real)
    n = b * seq
    wo = w_out[0].astype(BF16)
    h2, xh = _outproj(y_m.reshape(n, M_WIDTH), y_f.reshape(n, F_WIDTH), x.reshape(n, d),
                      wo[:M_WIDTH], wo[M_WIDTH:], norm_ffn[0].reshape(1, d), _tile(n, (512, 256, 128)))

    kh, kl = _split_bf16(peer_sub_keys[0].reshape(2 * PEER_HEADS, N_KEYS, PEER_HALF))
    r2, e2, n1, e1 = _peer_route(xh, peer_query[0].astype(BF16), kh, kl, _tile(n, (256, 128)))
    u_packed = _pack_bf16(peer_u[0], False, 1024)
    vt_packed = _pack_bf16(peer_v[0], True, 1024)
    out = _peer_experts(xh, u_packed, vt_packed, r2, e2, n1, e1, h2, _tile(n, (512, 256, 128)), 16)
    return out.reshape(b, seq, d)
```

```python
import functools

import jax
import jax.numpy as jnp
from jax import lax
from jax.experimental import pallas as pl
from jax.experimental.pallas import tpu as pltpu

F32 = jnp.float32
BF16 = jnp.bfloat16

EPS = 1e-6
N_META = 16
M_HEADS = 4
M_HEAD_DIM = 128
M_WIDTH = M_HEADS * M_HEAD_DIM
CONV_W = 4
F_HEADS = 8
F_HEAD_DIM = 64
F_WIDTH = F_HEADS * F_HEAD_DIM
PEER_HEADS = 8
N_KEYS = 128
PEER_TOPK = 16
PEER_HALF = 128

SEQ_BLOCK = 128
FOX_BLOCK = 256
GATE_LANES = 128
NEG_BIG = -1e30
VMEM_LIMIT = 56 * 1024 * 1024

_NT = (((1,), (1,)), ((), ()))
_TN = (((0,), (0,)), ((), ()))


def _params(*sem, flags=None):
    return pltpu.CompilerParams(dimension_semantics=sem, vmem_limit_bytes=VMEM_LIMIT, flags=flags)


def _split_bf16(a):
    hi = a.astype(BF16)
    lo = (a - hi.astype(F32)).astype(BF16)
    return hi, lo


def _dot(a, b):
    return jnp.dot(a, b, preferred_element_type=F32)


def _dot3(ah, al, bh, bl, dims=None):
    if dims is None:
        f = _dot
    else:
        f = lambda x, y: lax.dot_general(x, y, dims, preferred_element_type=F32)
    return f(ah, bh) + (f(ah, bl) + f(al, bh))


def _inproj_kernel(x_ref, head_ref, g_ref, wm_ref, wf_ref, fg_ref, wvt_ref, wgh_ref, wgl_ref,
                   zm_ref, zf_ref, vt_ref, zg_ref):
    h = jnp.where(pl.program_id(1) == 0, head_ref[...], x_ref[...])
    hn = h * lax.rsqrt(jnp.mean(h * h, axis=-1, keepdims=True) + EPS) * g_ref[...]
    hb, hl = _split_bf16(hn)
    zm_ref[...] = _dot(hb, wm_ref[...])
    zf = _dot(hb, wf_ref[...])
    low = lax.broadcasted_iota(jnp.int32, (zf.shape[0], 128), 1) < F_HEAD_DIM
    for c in range(zf.shape[1] // 128):
        x = zf[:, c * 128:(c + 1) * 128]
        x2 = x * x
        ms_lo = jnp.sum(jnp.where(low, x2, 0.0), axis=1, keepdims=True) * (1.0 / F_HEAD_DIM)
        ms_hi = jnp.sum(jnp.where(low, 0.0, x2), axis=1, keepdims=True) * (1.0 / F_HEAD_DIM)
        r = jnp.where(low, lax.rsqrt(ms_lo + EPS), lax.rsqrt(ms_hi + EPS))
        zf_ref[:, c * 128:(c + 1) * 128] = x * r * fg_ref[:, c * 128:(c + 1) * 128]
    vt_ref[...] = lax.dot_general(wvt_ref[...], hb, _NT, preferred_element_type=F32)
    zg_ref[...] = _dot3(hb, hl, wgh_ref[...], wgl_ref[...])


def _inproj(x, head, g, wm, wf, fg, wvt, wgh, wgl):
    b, seq, d = x.shape
    tm = head.shape[0]
    nb = seq // tm + 1
    lp = nb * tm
    full = lambda a: pl.BlockSpec(a.shape, lambda i, j: (0, 0))
    rows = lambda w: pl.BlockSpec((None, tm, w), lambda i, j: (i, j, 0))
    return pl.pallas_call(
        _inproj_kernel,
        grid=(b, nb),
        in_specs=[pl.BlockSpec((None, tm, d), lambda i, j: (i, jnp.maximum(j - 1, 0), 0)), full(head), full(g),
                  full(wm), full(wf), full(fg), full(wvt), full(wgh), full(wgl)],
        out_specs=[rows(wm.shape[1]), rows(wf.shape[1]),
                   pl.BlockSpec((wvt.shape[0], tm), lambda i, j: (0, i * nb + j)),
                   rows(GATE_LANES)],
        out_shape=[jax.ShapeDtypeStruct((b, lp, wm.shape[1]), F32),
                   jax.ShapeDtypeStruct((b, lp, wf.shape[1]), F32),
                   jax.ShapeDtypeStruct((wvt.shape[0], b * lp), F32),
                   jax.ShapeDtypeStruct((b, lp, GATE_LANES), F32)],
        compiler_params=_params("parallel", "arbitrary"),
        name="inproj",
    )(x, head, g, wm, wf, fg, wvt, wgh, wgl)


def _log_sigmoid(x):
    return jnp.minimum(x, 0.0) - jnp.log1p(jnp.exp(-jnp.abs(x)))


def _gates_kernel(zg_ref, bias_ref, gcol_ref, grow_ref):
    lp = zg_ref.shape[0]
    t = SEQ_BLOCK
    r = lax.broadcasted_iota(jnp.int32, (t, t), 0)
    c = lax.broadcasted_iota(jnp.int32, (t, t), 1)
    tri = jnp.where(r >= c, 1.0, 0.0).astype(BF16)
    lane = lax.broadcasted_iota(jnp.int32, (t, GATE_LANES), 1)
    carry = jnp.zeros((1, GATE_LANES), F32)
    for i in range(lp // t):
        pre = zg_ref[i * t:(i + 1) * t, :] + bias_ref[...]
        ls = _log_sigmoid(pre)
        l0 = ls.astype(BF16)
        r1 = ls - l0.astype(F32)
        l1 = r1.astype(BF16)
        l2 = (r1 - l1.astype(F32)).astype(BF16)
        cs = (_dot(tri, l0) + (_dot(tri, l1) + _dot(tri, l2))) + carry
        carry = cs[t - 1:t, :]
        out = jnp.where(lane < M_HEADS, pre, cs)
        gcol_ref[i * t:(i + 1) * t, :] = out
        grow_ref[:, i * t:(i + 1) * t] = out.T[:16, :]


def _gates(zg, bias):
    b, lp, _ = zg.shape
    return pl.pallas_call(
        _gates_kernel,
        grid=(b,),
        in_specs=[pl.BlockSpec((None, lp, GATE_LANES), lambda i: (i, 0, 0)),
                  pl.BlockSpec((1, GATE_LANES), lambda i: (0, 0))],
        out_specs=[pl.BlockSpec((None, lp, GATE_LANES), lambda i: (i, 0, 0)),
                   pl.BlockSpec((None, 16, lp), lambda i: (i, 0, 0))],
        out_shape=[jax.ShapeDtypeStruct((b, lp, GATE_LANES), F32),
                   jax.ShapeDtypeStruct((b, 16, lp), F32)],
        compiler_params=_params("parallel"),
        name="gates",
    )(zg, bias)


def _mlstm_kernel(has_prev, zm_ref, gcol_ref, grow_ref, gprev_ref, convw_ref, gain_ref, y_ref,
                  prevx_ref, ct_ref, n_ref, m_ref, pc_ref):
    j = pl.program_id(1)
    t = SEQ_BLOCK
    w2 = 2 * M_WIDTH

    @pl.when(j == 0)
    def _():
        prevx_ref[...] = jnp.zeros_like(prevx_ref)
        ct_ref[...] = jnp.zeros_like(ct_ref)
        n_ref[...] = jnp.zeros_like(n_ref)
        m_ref[...] = jnp.zeros_like(m_ref)
        pc_ref[...] = (jnp.broadcast_to(gprev_ref[7:8, :], pc_ref.shape) if has_prev
                       else jnp.zeros_like(pc_ref))

    xqk = zm_ref[:, :w2]
    prev = prevx_ref[...]
    rowi = lax.broadcasted_iota(jnp.int32, (t, w2), 0)
    acc = xqk * convw_ref[CONV_W - 1:CONV_W, :]
    for s in range(1, CONV_W):
        shifted = jnp.where(rowi < s, pltpu.roll(prev, s, 0), pltpu.roll(xqk, s, 0))
        acc = acc + shifted * convw_ref[CONV_W - 1 - s:CONV_W - s, :]
    prevx_ref[...] = xqk
    qk = acc * jax.nn.sigmoid(acc)

    gcol = gcol_ref[...]
    grow = grow_ref[...]
    rr = lax.broadcasted_iota(jnp.int32, (t, t), 0)
    cc = lax.broadcasted_iota(jnp.int32, (t, t), 1)
    causal = rr >= cc
    scale = M_HEAD_DIM ** -0.5

    heads = range(M_HEADS)
    hsl = [slice(h * M_HEAD_DIM, (h + 1) * M_HEAD_DIM) for h in heads]
    q = [qk[:, hsl[h]] for h in heads]
    k = [qk[:, M_WIDTH + h * M_HEAD_DIM:M_WIDTH + (h + 1) * M_HEAD_DIM] * scale for h in heads]
    qb = [x.astype(BF16) for x in q]
    kb = [x.astype(BF16) for x in k]
    vb = [zm_ref[:, w2 + h * M_HEAD_DIM:w2 + (h + 1) * M_HEAD_DIM].astype(BF16) for h in heads]
    ct = [ct_ref[h] for h in heads]
    n_row = [n_ref[h, 0:1, :] for h in heads]
    m_prev = [m_ref[h, 0:1, 0:1] for h in heads]

    qk_dot = [lax.dot_general(qb[h], kb[h], _NT, preferred_element_type=F32) for h in heads]
    q_ct = [_dot(qb[h], ct[h].astype(BF16)) for h in heads]

    li_c = [gcol[:, h:h + 1] for h in heads]
    li_r = [grow[h:h + 1, :] for h in heads]
    prev_cum = [pc_ref[0:1, M_HEADS + h:M_HEADS + h + 1] for h in heads]
    b_c = [gcol[:, M_HEADS + h:M_HEADS + h + 1] - prev_cum[h] for h in heads]
    b_r = [grow[M_HEADS + h:M_HEADS + h + 1, :] - prev_cum[h] for h in heads]
    dm = [jnp.where(causal, (b_c[h] - b_r[h]) + li_r[h], -jnp.inf) for h in heads]
    m_inter = [b_c[h] + m_prev[h] for h in heads]
    m_t = [jnp.maximum(m_inter[h], jnp.max(dm[h], axis=1, keepdims=True)) for h in heads]
    w_inter = [jnp.exp(m_inter[h] - m_t[h]) for h in heads]
    s_mat = [qk_dot[h] * jnp.exp(dm[h] - m_t[h]) for h in heads]

    num = [_dot(s_mat[h].astype(BF16), vb[h]) + w_inter[h] * q_ct[h] for h in heads]
    den = [jnp.sum(s_mat[h], axis=1, keepdims=True)
           + w_inter[h] * jnp.sum(q[h] * n_row[h], axis=1, keepdims=True) for h in heads]
    hh = [num[h] / jnp.maximum(jnp.abs(den[h]), jnp.exp(-m_t[h])) for h in heads]

    b_end = [b_c[h][t - 1:t, :] for h in heads]
    g_c = [(b_end[h] - b_c[h]) + li_c[h] for h in heads]
    g_r = [(b_end[h] - b_r[h]) + li_r[h] for h in heads]
    m_new = [jnp.maximum(b_end[h] + m_prev[h], jnp.max(g_r[h], axis=1, keepdims=True)) for h in heads]
    decay = [jnp.exp(b_end[h] + m_prev[h] - m_new[h]) for h in heads]
    kw = [k[h] * jnp.exp(g_c[h] - m_new[h]) for h in heads]
    for h in heads:
        ct_ref[h] = decay[h] * ct[h] + lax.dot_general(kw[h].astype(BF16), vb[h], _TN, preferred_element_type=F32)
        n_ref[h] = jnp.broadcast_to(decay[h] * n_row[h] + jnp.sum(kw[h], axis=0, keepdims=True), (8, M_HEAD_DIM))
        m_ref[h] = jnp.broadcast_to(m_new[h], (8, 128))

    for h in heads:
        o_pre = zm_ref[:, w2 + M_WIDTH + h * M_HEAD_DIM:w2 + M_WIDTH + (h + 1) * M_HEAD_DIM]
        hn = hh[h] * lax.rsqrt(jnp.mean(hh[h] * hh[h], axis=-1, keepdims=True) + EPS) * gain_ref[:, hsl[h]]
        y_ref[:, hsl[h]] = (hn * jax.nn.sigmoid(o_pre)).astype(y_ref.dtype)

    pc_ref[...] = jnp.broadcast_to(gcol[t - 1:t, :], (8, GATE_LANES))


def _mlstm(zm, gcol, grow, convw, gain, seq):
    b, lp, wz = zm.shape
    t = SEQ_BLOCK
    nb = seq // t + 1
    skip = lp // t - nb
    prev_rows = max(skip * t // 8 - 1, 0)
    return pl.pallas_call(
        functools.partial(_mlstm_kernel, skip > 0),
        grid=(b, nb),
        in_specs=[pl.BlockSpec((None, t, wz), lambda i, j: (i, j + skip, 0)),
                  pl.BlockSpec((None, t, GATE_LANES), lambda i, j: (i, j + skip, 0)),
                  pl.BlockSpec((None, 16, t), lambda i, j: (i, 0, j + skip)),
                  pl.BlockSpec((None, 8, GATE_LANES), lambda i, j: (i, prev_rows, 0)),
                  pl.BlockSpec(convw.shape, lambda i, j: (0, 0)),
                  pl.BlockSpec(gain.shape, lambda i, j: (0, 0))],
        out_specs=pl.BlockSpec((None, t, M_WIDTH), lambda i, j: (i, jnp.maximum(j - 1, 0), 0)),
        out_shape=jax.ShapeDtypeStruct((b, seq, M_WIDTH), BF16),
        scratch_shapes=[pltpu.VMEM((t, 2 * M_WIDTH), F32),
                        pltpu.VMEM((M_HEADS, M_HEAD_DIM, M_HEAD_DIM), F32),
                        pltpu.VMEM((M_HEADS, 8, M_HEAD_DIM), F32),
                        pltpu.VMEM((M_HEADS, 8, 128), F32),
                        pltpu.VMEM((8, GATE_LANES), F32)],
        compiler_params=_params("parallel", "arbitrary"),
        name="mlstm",
    )(zm, gcol, grow, gcol, convw, gain)


_FOX_KDIM = 128
_LOG2E = 1.4426950408889634


def _split3(c):
    c0 = c.astype(BF16).astype(F32)
    r1 = c - c0
    c1 = r1.astype(BF16).astype(F32)
    c2 = (r1 - c1).astype(BF16).astype(F32)
    return c0, c1, c2


def _fox_kernel(first_real, q_ref, k_ref, v_ref, gcol_ref, y_ref,
                kx_ref, vt_ref, qx_ref, *state_refs):
    qi = pl.program_id(1) + 1
    t = FOX_BLOCK
    d = F_HEAD_DIM
    lp = k_ref.shape[0]
    elane = lax.broadcasted_iota(jnp.int32, (t, _FOX_KDIM - d), 1)

    @pl.when(qi == 1)
    def _():
        def prep(c, carry):
            r0 = pl.multiple_of(c * t, t)
            rows = pl.ds(r0, t)
            vt_ref[:, rows] = v_ref[:, rows].astype(BF16)
            g = gcol_ref[rows, :]
            for h in range(F_HEADS):
                kn = k_ref[rows, h * d:(h + 1) * d]
                c0, c1, c2 = _split3(g[:, 2 * M_HEADS + h:2 * M_HEADS + h + 1] * _LOG2E)
                extra = jnp.where(elane < 3, 1.0,
                                  jnp.where(elane == 3, -c0, jnp.where(elane == 4, -c1,
                                                                       jnp.where(elane == 5, -c2, 0.0))))
                kx_ref[rows, h * _FOX_KDIM:(h + 1) * _FOX_KDIM] = jnp.concatenate([kn, extra], axis=1).astype(BF16)
            return carry
        lax.fori_loop(0, lp // t, prep, 0)

    scale = d ** -0.5 * _LOG2E
    gq_rows = gcol_ref[pl.ds(pl.multiple_of(qi * t, t), t), :] * _LOG2E
    for h in range(F_HEADS):
        qn = q_ref[:, h * d:(h + 1) * d] * scale
        c0, c1, c2 = _split3(gq_rows[:, 2 * M_HEADS + h:2 * M_HEADS + h + 1])
        extra = jnp.where(elane == 0, c0, jnp.where(elane == 1, c1, jnp.where(elane == 2, c2,
                                                                              jnp.where(elane < 6, 1.0, 0.0))))
        qx_ref[h] = jnp.concatenate([qn, extra], axis=1).astype(BF16)

    acc_refs, m_refs, l_refs = (state_refs[i * F_HEADS:(i + 1) * F_HEADS] for i in range(3))
    for h in range(F_HEADS):
        m_refs[h][...] = jnp.full_like(m_refs[h], NEG_BIG)
        l_refs[h][...] = jnp.zeros_like(l_refs[h])
        acc_refs[h][...] = jnp.zeros_like(acc_refs[h])

    kk = lax.broadcasted_iota(jnp.int32, (t, t), 0)
    qq = lax.broadcasted_iota(jnp.int32, (t, t), 1)

    def step(jb, mask):
        start = pl.multiple_of(jb * t, t)
        scores = []
        for h in range(F_HEADS):
            kx = kx_ref[pl.ds(start, t), h * _FOX_KDIM:(h + 1) * _FOX_KDIM]
            scores.append(lax.dot_general(kx, qx_ref[h], _NT, preferred_element_type=F32))
        probs, alphas = [], []
        for h in range(F_HEADS):
            s = scores[h] if mask is None else jnp.where(mask, scores[h], NEG_BIG)
            m_old = m_refs[h][...]
            m_new = jnp.maximum(m_old, jnp.max(s, axis=0, keepdims=True))
            p = jnp.exp2(s - m_new)
            alpha = jnp.exp2(m_old - m_new)
            l_refs[h][...] = alpha * l_refs[h][...] + jnp.sum(p, axis=0, keepdims=True)
            m_refs[h][...] = m_new
            probs.append(p.astype(BF16))
            alphas.append(alpha)
        for h in range(F_HEADS):
            acc_refs[h][...] = alphas[h] * acc_refs[h][...] + _dot(vt_ref[h * d:(h + 1) * d, pl.ds(start, t)],
                                                                  probs[h])

    step(0, kk >= first_real)
    lax.fori_loop(1, qi, lambda jb, c: (step(jb, None), c)[1], 0)
    step(qi, kk <= qq)
    yt = jnp.concatenate([acc_refs[h][...] / l_refs[h][...] for h in range(F_HEADS)], axis=0)
    y_ref[...] = yt.T.astype(y_ref.dtype)


def _fox(zf, vt, gcol, first_real):
    b, lp, _ = zf.shape
    t = FOX_BLOCK
    nb = lp // t
    return pl.pallas_call(
        functools.partial(_fox_kernel, first_real),
        grid=(b, nb - 1),
        in_specs=[pl.BlockSpec((None, t, F_WIDTH), lambda i, j: (i, j + 1, 0)),
                  pl.BlockSpec((None, lp, F_WIDTH), lambda i, j: (i, 0, 1)),
                  pl.BlockSpec((F_WIDTH, lp), lambda i, j: (0, i)),
                  pl.BlockSpec((None, lp, GATE_LANES), lambda i, j: (i, 0, 0))],
        out_specs=pl.BlockSpec((None, t, F_WIDTH), lambda i, j: (i, j, 0)),
        out_shape=jax.ShapeDtypeStruct((b, lp - t, F_WIDTH), BF16),
        scratch_shapes=[pltpu.VMEM((lp, F_HEADS * _FOX_KDIM), BF16),
                        pltpu.VMEM((F_WIDTH, lp), BF16),
                        pltpu.VMEM((F_HEADS, t, _FOX_KDIM), BF16)]
                       + [pltpu.VMEM((F_HEAD_DIM, t), F32)] * F_HEADS
                       + [pltpu.VMEM((1, t), F32)] * F_HEADS
                       + [pltpu.VMEM((1, t), F32)] * F_HEADS,
        compiler_params=_params("parallel", "arbitrary"),
        name="fox",
    )(zf, zf, vt, gcol)


def _outproj_kernel(ym_ref, yf_ref, x_ref, wm_ref, wf_ref, g_ref, h2_ref, xn_ref):
    h2 = x_ref[...] + (_dot(ym_ref[...], wm_ref[...]) + _dot(yf_ref[...], wf_ref[...]))
    h2_ref[...] = h2
    xn = h2 * lax.rsqrt(jnp.mean(h2 * h2, axis=-1, keepdims=True) + EPS) * g_ref[...]
    xn_ref[...] = xn.astype(BF16)


def _outproj(ym, yf, x2d, wm, wf, g, tm):
    n, d = x2d.shape
    row = lambda w: pl.BlockSpec((tm, w), lambda i: (i, 0))
    full = lambda a: pl.BlockSpec(a.shape, lambda i: (0, 0))
    return pl.pallas_call(
        _outproj_kernel,
        grid=(n // tm,),
        in_specs=[row(ym.shape[1]), row(yf.shape[1]), row(d), full(wm), full(wf), full(g)],
        out_specs=[row(d), row(d)],
        out_shape=[jax.ShapeDtypeStruct((n, d), F32),
                   jax.ShapeDtypeStruct((n, d), BF16)],
        compiler_params=_params("parallel"),
        name="outproj",
    )(ym, yf, x2d, wm, wf, g)


_TOP = PEER_TOPK + 1
_TOP_ROWS = 24


def _merge_exchange_network(n):
    t = max(1, (n - 1).bit_length())
    pairs = []
    p = 1 << (t - 1)
    while p > 0:
        q, r, d = 1 << (t - 1), 0, p
        while d > 0:
            pairs += [(i, i + d) for i in range(n - d) if (i & p) == r]
            d, q, r = q - p, q >> 1, p
        p >>= 1
    return pairs


def _pop_largest(s):
    m = s.shape[0] // 8
    cols = [s[8 * i:8 * (i + 1), :] for i in range(m)]
    for i, j in _merge_exchange_network(m):
        cols[i], cols[j] = jnp.maximum(cols[i], cols[j]), jnp.minimum(cols[i], cols[j])
    values, counts = [], []
    for it in range(_TOP):
        best = jnp.max(cols[0], axis=0, keepdims=True)
        eq = cols[0] == best
        values.append(best)
        counts.append(jnp.sum(jnp.where(eq, 1.0, 0.0), axis=0, keepdims=True))
        live = min(m, _TOP - it)
        for d in range(live - 1):
            cols[d] = jnp.where(eq, cols[d + 1], cols[d])
        if live == m:
            cols[m - 1] = jnp.where(eq, -jnp.inf, cols[m - 1])
    return values, counts


def _top_values(s):
    tb = s.shape[1]
    rank = lax.broadcasted_iota(jnp.int32, (_TOP_ROWS, tb), 0).astype(F32)
    vals = None
    cnt = jnp.zeros((1, tb), F32)
    for best, k in zip(*_pop_largest(s)):
        vals = jnp.where(rank >= cnt, best, -jnp.inf if vals is None else vals)
        cnt = cnt + k
    return jnp.where(rank < _TOP, vals, -jnp.inf)


def _pair_threshold(a, b):
    k = PEER_TOPK
    slabs = [a[0:1, :] + b[0:_TOP_ROWS, :]]
    for r in range(1, 8):
        slabs.append(a[r:r + 1, :] + b[0:8, :])
    slabs.append(a[8:_TOP_ROWS, :] + b[0:1, :])
    top = a[0:1, :] + b[0:1, :]
    cnt = jnp.zeros_like(top)
    v_k = top
    v_k1 = top
    z = jnp.zeros_like(top)
    for best, n_eq in zip(*_pop_largest(jnp.concatenate(slabs, axis=0))):
        take = jnp.clip(k - cnt, 0.0, n_eq)
        z = z + take * jnp.exp(best - top)
        v_k = jnp.where(cnt < k, best, v_k)
        v_k1 = jnp.where(cnt < k + 1, best, v_k1)
        cnt = cnt + n_eq
    return 0.5 * (v_k + v_k1), top, z


def _count_leading(sorted_rows, pred):
    out = None
    for c in range(_TOP):
        hit = pred(sorted_rows[c:c + 1, :])
        out = jnp.where(hit, c + 1.0, 0.0 if out is None else out)
    return out


def _peer_route_kernel(xn_ref, wq_ref, kh_ref, kl_ref, r2_ref, e2_ref, n_ref, e1_ref):
    q = _dot(xn_ref[...], wq_ref[...])
    for h in range(PEER_HEADS):
        st = []
        for p in range(2):
            i = 2 * h + p
            qh, ql = _split_bf16(q[:, i * PEER_HALF:(i + 1) * PEER_HALF])
            st.append(_dot3(kh_ref[i], kl_ref[i], qh, ql, _NT))
        a = _top_values(st[0])
        b = _top_values(st[1])
        thr, top, z = _pair_threshold(a, b)
        need = thr - st[0]
        above2 = _count_leading(b, lambda row: row > st[1])
        n1 = _count_leading(b, lambda row: row >= need)
        r2_ref[h] = pltpu.bitcast(above2.astype(BF16), jnp.uint32)
        e2_ref[h] = pltpu.bitcast(jnp.exp(st[1] - b[0:1, :]).astype(BF16), jnp.uint32)
        n_ref[h] = n1
        e1_ref[h] = jnp.exp(st[0] - a[0:1, :]) * (0.5 / z)


def _peer_route(xn, wq, kh, kl, tb):
    n, d = xn.shape
    row = pl.BlockSpec((tb, d), lambda i: (i, 0))
    full2 = lambda a: pl.BlockSpec(a.shape, lambda i: (0, 0))
    full3 = lambda a: pl.BlockSpec(a.shape, lambda i: (0, 0, 0))
    out = lambda rows: pl.BlockSpec((PEER_HEADS, rows, tb), lambda i: (0, 0, i))
    shape = lambda rows, dt: jax.ShapeDtypeStruct((PEER_HEADS, rows, n), dt)
    return pl.pallas_call(
        _peer_route_kernel,
        grid=(n // tb,),
        in_specs=[row, full2(wq), full3(kh), full3(kl)],
        out_specs=[out(N_KEYS // 2), out(N_KEYS // 2), out(N_KEYS), out(N_KEYS)],
        out_shape=[shape(N_KEYS // 2, jnp.uint32), shape(N_KEYS // 2, jnp.uint32),
                   shape(N_KEYS, F32), shape(N_KEYS, F32)],
        compiler_params=_params("parallel"),
        name="peer_route",
    )(xn, wq, kh, kl)


def _twice_gelu(x):
    return x * (1.0 + lax.erf(x * (2.0 ** -0.5)))


_PACK = 16


_ROWS_PER_PIECE = 4
_PIECES_PER_CHUNK = 1


def _pack_rows_kernel(x_ref, o_ref):
    o_ref[...] = pltpu.bitcast(x_ref[...].astype(BF16), jnp.uint32)


def _pack_rows_t_kernel(x_ref, o_ref):
    o_ref[...] = pltpu.bitcast(x_ref[...].T.astype(BF16), jnp.uint32)


def _pack_bf16(x, transpose, rows):
    n, c = x.shape
    if transpose:
        kern, out_shape, out_spec = _pack_rows_t_kernel, (c // 2, n), pl.BlockSpec((c // 2, rows), lambda i: (0, i))
    else:
        kern, out_shape, out_spec = _pack_rows_kernel, (n // 2, c), pl.BlockSpec((rows // 2, c), lambda i: (i, 0))
    return pl.pallas_call(
        kern,
        grid=(n // rows,),
        in_specs=[pl.BlockSpec((rows, c), lambda i: (i, 0))],
        out_specs=out_spec,
        out_shape=jax.ShapeDtypeStruct(out_shape, jnp.uint32),
        compiler_params=_params("parallel"),
        name="pack_t" if transpose else "pack",
    )(x)


def _peer_expert_kernel(ib, xn_ref, u_ref, vt_ref, r2_ref, e2_ref, n_ref, e1_ref, h2_ref, o_ref,
                        acc_ref, w_ref, act_ref):
    e = pl.program_id(1)
    tb = xn_ref.shape[0]
    d = 2 * vt_ref.shape[0]
    piece = _ROWS_PER_PIECE * N_KEYS
    chunk = _PIECES_PER_CHUNK * piece
    n_chunks = ib * N_KEYS // chunk
    out_rows = d // _PIECES_PER_CHUNK

    @pl.when(e == 0)
    def _():
        acc_ref[...] = jnp.zeros_like(acc_ref)

    def act_piece(c, k):
        lo = c * chunk + k * piece
        act_ref[c % 2, k * piece:(k + 1) * piece, :] = lax.dot_general(
            pltpu.bitcast(u_ref[lo // 2:(lo + piece) // 2, :], BF16), xn_ref[...], _NT,
            preferred_element_type=F32)

    def out_piece(c, k):
        rows = slice(k * out_rows, (k + 1) * out_rows)
        vt = pltpu.bitcast(vt_ref[k * out_rows // 2:(k + 1) * out_rows // 2, c * chunk:(c + 1) * chunk], BF16)
        acc_ref[rows, :] += _dot(vt, w_ref[c * chunk:(c + 1) * chunk, :])

    def gate_piece(c, k):
        i0 = (c * chunk + k * piece) // N_KEYS
        for tc in range(tb // 128):
            cols = slice(tc * 128, (tc + 1) * 128)
            bcast = lambda ref, h, i: jnp.broadcast_to(ref[h, i:i + 1, cols], (_PACK, 128)).astype(BF16)
            gates = [[None] * (N_KEYS // _PACK) for _ in range(_ROWS_PER_PIECE)]
            for h in range(PEER_HEADS):
                n1 = [bcast(n_ref, h, i0 + r) for r in range(_ROWS_PER_PIECE)]
                e1 = [bcast(e1_ref, h, i0 + r) for r in range(_ROWS_PER_PIECE)]
                for g in range(N_KEYS // _PACK):
                    ks = slice(g * _PACK // 2, (g + 1) * _PACK // 2)
                    above2 = pltpu.bitcast(r2_ref[h, ks, cols], BF16)
                    e2 = pltpu.bitcast(e2_ref[h, ks, cols], BF16)
                    for r in range(_ROWS_PER_PIECE):
                        term = e1[r] * jnp.where(above2 < n1[r], e2, jnp.zeros_like(e2))
                        gates[r][g] = term if gates[r][g] is None else gates[r][g] + term
            for r in range(_ROWS_PER_PIECE):
                for g in range(N_KEYS // _PACK):
                    lo = k * piece + r * N_KEYS + g * _PACK
                    act = act_ref[c % 2, lo:lo + _PACK, cols]
                    w_ref[c * chunk + lo:c * chunk + lo + _PACK, cols] = gates[r][g] * _twice_gelu(act).astype(BF16)

    for k in range(_PIECES_PER_CHUNK):
        act_piece(0, k)
    for c in range(n_chunks):
        for k in range(_PIECES_PER_CHUNK):
            if c + 1 < n_chunks:
                act_piece(c + 1, k)
            if c > 0:
                out_piece(c - 1, k)
            gate_piece(c, k)
    for k in range(_PIECES_PER_CHUNK):
        out_piece(n_chunks - 1, k)

    @pl.when(e == pl.num_programs(1) - 1)
    def _():
        o_ref[...] = h2_ref[...] + acc_ref[...].T


def _peer_experts(xn, u_packed, vt_packed, r2, e2, n1, e1, h2, tb, ib):
    n, d = xn.shape
    ne = 2 * u_packed.shape[0]
    eb = ib * N_KEYS
    chunk = _PIECES_PER_CHUNK * _ROWS_PER_PIECE * N_KEYS
    assert eb % chunk == 0 and tb % 128 == 0
    gate_full = pl.BlockSpec((PEER_HEADS, N_KEYS // 2, tb), lambda t, e: (0, 0, t))
    gate_rows = pl.BlockSpec((PEER_HEADS, ib, tb), lambda t, e: (0, e, t))
    return pl.pallas_call(
        functools.partial(_peer_expert_kernel, ib),
        grid=(n // tb, ne // eb),
        in_specs=[pl.BlockSpec((tb, d), lambda t, e: (t, 0)),
                  pl.BlockSpec((eb // 2, d), lambda t, e: (e, 0)),
                  pl.BlockSpec((d // 2, eb), lambda t, e: (0, e)),
                  gate_full, gate_full, gate_rows, gate_rows,
                  pl.BlockSpec((tb, d), lambda t, e: (t, 0))],
        out_specs=pl.BlockSpec((tb, d), lambda t, e: (t, 0)),
        out_shape=jax.ShapeDtypeStruct((n, d), F32),
        scratch_shapes=[pltpu.VMEM((d, tb), F32),
                        pltpu.VMEM((eb, tb), BF16),
                        pltpu.VMEM((2, chunk, tb), F32)],
        compiler_params=_params("parallel", "arbitrary"),
        name="peer_experts",
    )(xn, u_packed, vt_packed, r2, e2, n1, e1, h2)


def _tile(n, prefer):
    for t in prefer:
        if n % t == 0:
            return t
    raise ValueError(f"no tile for {n}")


def kernel(x, meta_tokens, norm_mix, w_in, conv_qk, b_igate, b_fgate_m, m_out_norm, b_fgate_f,
           f_q_norm, f_k_norm, w_out, norm_ffn, peer_query, peer_sub_keys, peer_u, peer_v):
    assert w_in.shape[0] == 1, "single-layer block"
    b, seq, d = x.shape
    t = FOX_BLOCK
    assert seq % t == 0 and N_META <= SEQ_BLOCK and t % SEQ_BLOCK == 0
    first_real = t - N_META

    head = jnp.concatenate([jnp.zeros((first_real, d), x.dtype), meta_tokens.astype(x.dtype)], axis=0)
    w = w_in[0]
    o = 0
    cols = {}
    for name, size in (("mq", M_WIDTH), ("mk", M_WIDTH), ("mv", M_WIDTH), ("mo", M_WIDTH), ("mi", M_HEADS),
                       ("mf", M_HEADS), ("fq", F_WIDTH), ("fk", F_WIDTH), ("fv", F_WIDTH), ("ff", F_HEADS)):
        cols[name] = w[:, o:o + size]
        o += size
    w_m = jnp.concatenate([cols["mq"], cols["mk"], cols["mv"], cols["mo"]], axis=1).astype(BF16)
    w_f = jnp.concatenate([cols["fq"], cols["fk"]], axis=1).astype(BF16)
    w_vt = cols["fv"].T.astype(BF16)
    n_gate = 2 * M_HEADS + F_HEADS
    w_g = jnp.concatenate([cols["mi"], cols["mf"], cols["ff"], jnp.zeros((d, GATE_LANES - n_gate), F32)], axis=1)
    w_gh, w_gl = _split_bf16(w_g)
    gate_bias = jnp.concatenate([b_igate[0], b_fgate_m[0], b_fgate_f[0],
                                 jnp.zeros((GATE_LANES - n_gate,), F32)]).reshape(1, GATE_LANES)

    f_gain = jnp.concatenate([jnp.tile(f_q_norm[0], F_HEADS), jnp.tile(f_k_norm[0], F_HEADS)]).reshape(1, 2 * F_WIDTH)
    zm, zf, vt, zg = _inproj(x, head, norm_mix[0].reshape(1, d), w_m, w_f, f_gain, w_vt, w_gh, w_gl)
    gcol, grow = _gates(zg, gate_bias)
    y_m = _mlstm(zm, gcol, grow, conv_qk[0], m_out_norm[0].reshape(1, M_WIDTH), seq)
    y_f = _fox(zf, vt, gcol, first_real)
    n = b * seq
    wo = w_out[0].astype(BF16)
    h2, xh = _outproj(y_m.reshape(n, M_WIDTH), y_f.reshape(n, F_WIDTH), x.reshape(n, d),
                      wo[:M_WIDTH], wo[M_WIDTH:], norm_ffn[0].reshape(1, d), _tile(n, (512, 256, 128)))

    kh, kl = _split_bf16(peer_sub_keys[0].reshape(2 * PEER_HEADS, N_KEYS, PEER_HALF))
    r2, e2, n1, e1 = _peer_route(xh, peer_query[0].astype(BF16), kh, kl, _tile(n, (256, 128)))
    u_packed = _pack_bf16(peer_u[0], False, 1024)
    vt_packed = _pack_bf16(peer_v[0], True, 1024)
    out = _peer_experts(xh, u_packed, vt_packed, r2, e2, n1, e1, h2, _tile(n, (512, 256, 128)), 16)
    return out.reshape(b, seq, d)
```

```python
import functools

import jax
import jax.numpy as jnp
from jax import lax
from jax.experimental import pallas as pl
from jax.experimental.pallas import tpu as pltpu

F32 = jnp.float32
BF16 = jnp.bfloat16

EPS = 1e-6
N_META = 16
M_HEADS = 4
M_HEAD_DIM = 128
M_WIDTH = M_HEADS * M_HEAD_DIM
CONV_W = 4
F_HEADS = 8
F_HEAD_DIM = 64
F_WIDTH = F_HEADS * F_HEAD_DIM
PEER_HEADS = 8
N_KEYS = 128
PEER_TOPK = 16
PEER_HALF = 128

SEQ_BLOCK = 128
FOX_BLOCK = 256
GATE_LANES = 128
NEG_BIG = -1e30
VMEM_LIMIT = 56 * 1024 * 1024

_NT = (((1,), (1,)), ((), ()))
_TN = (((0,), (0,)), ((), ()))


def _params(*sem, flags=None):
    return pltpu.CompilerParams(dimension_semantics=sem, vmem_limit_bytes=VMEM_LIMIT, flags=flags)


def _split_bf16(a):
    hi = a.astype(BF16)
    lo = (a - hi.astype(F32)).astype(BF16)
    return hi, lo


def _dot(a, b):
    return jnp.dot(a, b, preferred_element_type=F32)


def _dot3(ah, al, bh, bl, dims=None):
    if dims is None:
        f = _dot
    else:
        f = lambda x, y: lax.dot_general(x, y, dims, preferred_element_type=F32)
    return f(ah, bh) + (f(ah, bl) + f(al, bh))


def _inproj_kernel(x_ref, head_ref, g_ref, wm_ref, wf_ref, fg_ref, wvt_ref, wgh_ref, wgl_ref,
                   zm_ref, zf_ref, vt_ref, zg_ref):
    h = jnp.where(pl.program_id(1) == 0, head_ref[...], x_ref[...])
    hn = h * lax.rsqrt(jnp.mean(h * h, axis=-1, keepdims=True) + EPS) * g_ref[...]
    hb, hl = _split_bf16(hn)
    zm_ref[...] = _dot(hb, wm_ref[...])
    zf = _dot(hb, wf_ref[...])
    low = lax.broadcasted_iota(jnp.int32, (zf.shape[0], 128), 1) < F_HEAD_DIM
    for c in range(zf.shape[1] // 128):
        x = zf[:, c * 128:(c + 1) * 128]
        x2 = x * x
        ms_lo = jnp.sum(jnp.where(low, x2, 0.0), axis=1, keepdims=True) * (1.0 / F_HEAD_DIM)
        ms_hi = jnp.sum(jnp.where(low, 0.0, x2), axis=1, keepdims=True) * (1.0 / F_HEAD_DIM)
        r = jnp.where(low, lax.rsqrt(ms_lo + EPS), lax.rsqrt(ms_hi + EPS))
        zf_ref[:, c * 128:(c + 1) * 128] = x * r * fg_ref[:, c * 128:(c + 1) * 128]
    vt_ref[...] = lax.dot_general(wvt_ref[...], hb, _NT, preferred_element_type=F32)
    zg_ref[...] = _dot3(hb, hl, wgh_ref[...], wgl_ref[...])


def _inproj(x, head, g, wm, wf, fg, wvt, wgh, wgl):
    b, seq, d = x.shape
    tm = head.shape[0]
    nb = seq // tm + 1
    lp = nb * tm
    full = lambda a: pl.BlockSpec(a.shape, lambda i, j: (0, 0))
    rows = lambda w: pl.BlockSpec((None, tm, w), lambda i, j: (i, j, 0))
    return pl.pallas_call(
        _inproj_kernel,
        grid=(b, nb),
        in_specs=[pl.BlockSpec((None, tm, d), lambda i, j: (i, jnp.maximum(j - 1, 0), 0)), full(head), full(g),
                  full(wm), full(wf), full(fg), full(wvt), full(wgh), full(wgl)],
        out_specs=[rows(wm.shape[1]), rows(wf.shape[1]),
                   pl.BlockSpec((wvt.shape[0], tm), lambda i, j: (0, i * nb + j)),
                   rows(GATE_LANES)],
        out_shape=[jax.ShapeDtypeStruct((b, lp, wm.shape[1]), F32),
                   jax.ShapeDtypeStruct((b, lp, wf.shape[1]), F32),
                   jax.ShapeDtypeStruct((wvt.shape[0], b * lp), F32),
                   jax.ShapeDtypeStruct((b, lp, GATE_LANES), F32)],
        compiler_params=_params("parallel", "arbitrary"),
        name="inproj",
    )(x, head, g, wm, wf, fg, wvt, wgh, wgl)


def _log_sigmoid(x):
    return jnp.minimum(x, 0.0) - jnp.log1p(jnp.exp(-jnp.abs(x)))


def _gates_kernel(zg_ref, bias_ref, gcol_ref, grow_ref):
    lp = zg_ref.shape[0]
    t = SEQ_BLOCK
    r = lax.broadcasted_iota(jnp.int32, (t, t), 0)
    c = lax.broadcasted_iota(jnp.int32, (t, t), 1)
    tri = jnp.where(r >= c, 1.0, 0.0).astype(BF16)
    lane = lax.broadcasted_iota(jnp.int32, (t, GATE_LANES), 1)
    carry = jnp.zeros((1, GATE_LANES), F32)
    for i in range(lp // t):
        pre = zg_ref[i * t:(i + 1) * t, :] + bias_ref[...]
        ls = _log_sigmoid(pre)
        l0 = ls.astype(BF16)
        r1 = ls - l0.astype(F32)
        l1 = r1.astype(BF16)
        l2 = (r1 - l1.astype(F32)).astype(BF16)
        cs = (_dot(tri, l0) + (_dot(tri, l1) + _dot(tri, l2))) + carry
        carry = cs[t - 1:t, :]
        out = jnp.where(lane < M_HEADS, pre, cs)
        gcol_ref[i * t:(i + 1) * t, :] = out
        grow_ref[:, i * t:(i + 1) * t] = out.T[:16, :]


def _gates(zg, bias):
    b, lp, _ = zg.shape
    return pl.pallas_call(
        _gates_kernel,
        grid=(b,),
        in_specs=[pl.BlockSpec((None, lp, GATE_LANES), lambda i: (i, 0, 0)),
                  pl.BlockSpec((1, GATE_LANES), lambda i: (0, 0))],
        out_specs=[pl.BlockSpec((None, lp, GATE_LANES), lambda i: (i, 0, 0)),
                   pl.BlockSpec((None, 16, lp), lambda i: (i, 0, 0))],
        out_shape=[jax.ShapeDtypeStruct((b, lp, GATE_LANES), F32),
                   jax.ShapeDtypeStruct((b, 16, lp), F32)],
        compiler_params=_params("parallel"),
        name="gates",
    )(zg, bias)


def _mlstm_kernel(has_prev, zm_ref, gcol_ref, grow_ref, gprev_ref, convw_ref, gain_ref, y_ref,
                  prevx_ref, ct_ref, n_ref, m_ref, pc_ref):
    j = pl.program_id(1)
    t = SEQ_BLOCK
    w2 = 2 * M_WIDTH

    @pl.when(j == 0)
    def _():
        prevx_ref[...] = jnp.zeros_like(prevx_ref)
        ct_ref[...] = jnp.zeros_like(ct_ref)
        n_ref[...] = jnp.zeros_like(n_ref)
        m_ref[...] = jnp.zeros_like(m_ref)
        pc_ref[...] = (jnp.broadcast_to(gprev_ref[7:8, :], pc_ref.shape) if has_prev
                       else jnp.zeros_like(pc_ref))

    xqk = zm_ref[:, :w2]
    prev = prevx_ref[...]
    rowi = lax.broadcasted_iota(jnp.int32, (t, w2), 0)
    acc = xqk * convw_ref[CONV_W - 1:CONV_W, :]
    for s in range(1, CONV_W):
        shifted = jnp.where(rowi < s, pltpu.roll(prev, s, 0), pltpu.roll(xqk, s, 0))
        acc = acc + shifted * convw_ref[CONV_W - 1 - s:CONV_W - s, :]
    prevx_ref[...] = xqk
    qk = acc * jax.nn.sigmoid(acc)

    gcol = gcol_ref[...]
    grow = grow_ref[...]
    rr = lax.broadcasted_iota(jnp.int32, (t, t), 0)
    cc = lax.broadcasted_iota(jnp.int32, (t, t), 1)
    causal = rr >= cc
    scale = M_HEAD_DIM ** -0.5

    heads = range(M_HEADS)
    hsl = [slice(h * M_HEAD_DIM, (h + 1) * M_HEAD_DIM) for h in heads]
    q = [qk[:, hsl[h]] for h in heads]
    k = [qk[:, M_WIDTH + h * M_HEAD_DIM:M_WIDTH + (h + 1) * M_HEAD_DIM] * scale for h in heads]
    qb = [x.astype(BF16) for x in q]
    kb = [x.astype(BF16) for x in k]
    vb = [zm_ref[:, w2 + h * M_HEAD_DIM:w2 + (h + 1) * M_HEAD_DIM].astype(BF16) for h in heads]
    ct = [ct_ref[h] for h in heads]
    n_row = [n_ref[h, 0:1, :] for h in heads]
    m_prev = [m_ref[h, 0:1, 0:1] for h in heads]

    qk_dot = [lax.dot_general(qb[h], kb[h], _NT, preferred_element_type=F32) for h in heads]
    q_ct = [_dot(qb[h], ct[h].astype(BF16)) for h in heads]

    li_c = [gcol[:, h:h + 1] for h in heads]
    li_r = [grow[h:h + 1, :] for h in heads]
    prev_cum = [pc_ref[0:1, M_HEADS + h:M_HEADS + h + 1] for h in heads]
    b_c = [gcol[:, M_HEADS + h:M_HEADS + h + 1] - prev_cum[h] for h in heads]
    b_r = [grow[M_HEADS + h:M_HEADS + h + 1, :] - prev_cum[h] for h in heads]
    dm = [jnp.where(causal, (b_c[h] - b_r[h]) + li_r[h], -jnp.inf) for h in heads]
    m_inter = [b_c[h] + m_prev[h] for h in heads]
    m_t = [jnp.maximum(m_inter[h], jnp.max(dm[h], axis=1, keepdims=True)) for h in heads]
    w_inter = [jnp.exp(m_inter[h] - m_t[h]) for h in heads]
    s_mat = [qk_dot[h] * jnp.exp(dm[h] - m_t[h]) for h in heads]

    num = [_dot(s_mat[h].astype(BF16), vb[h]) + w_inter[h] * q_ct[h] for h in heads]
    den = [jnp.sum(s_mat[h], axis=1, keepdims=True)
           + w_inter[h] * jnp.sum(q[h] * n_row[h], axis=1, keepdims=True) for h in heads]
    hh = [num[h] / jnp.maximum(jnp.abs(den[h]), jnp.exp(-m_t[h])) for h in heads]

    b_end = [b_c[h][t - 1:t, :] for h in heads]
    g_c = [(b_end[h] - b_c[h]) + li_c[h] for h in heads]
    g_r = [(b_end[h] - b_r[h]) + li_r[h] for h in heads]
    m_new = [jnp.maximum(b_end[h] + m_prev[h], jnp.max(g_r[h], axis=1, keepdims=True)) for h in heads]
    decay = [jnp.exp(b_end[h] + m_prev[h] - m_new[h]) for h in heads]
    kw = [k[h] * jnp.exp(g_c[h] - m_new[h]) for h in heads]
    for h in heads:
        ct_ref[h] = decay[h] * ct[h] + lax.dot_general(kw[h].astype(BF16), vb[h], _TN, preferred_element_type=F32)
        n_ref[h] = jnp.broadcast_to(decay[h] * n_row[h] + jnp.sum(kw[h], axis=0, keepdims=True), (8, M_HEAD_DIM))
        m_ref[h] = jnp.broadcast_to(m_new[h], (8, 128))

    for h in heads:
        o_pre = zm_ref[:, w2 + M_WIDTH + h * M_HEAD_DIM:w2 + M_WIDTH + (h + 1) * M_HEAD_DIM]
        hn = hh[h] * lax.rsqrt(jnp.mean(hh[h] * hh[h], axis=-1, keepdims=True) + EPS) * gain_ref[:, hsl[h]]
        y_ref[:, hsl[h]] = (hn * jax.nn.sigmoid(o_pre)).astype(y_ref.dtype)

    pc_ref[...] = jnp.broadcast_to(gcol[t - 1:t, :], (8, GATE_LANES))


def _mlstm(zm, gcol, grow, convw, gain, seq):
    b, lp, wz = zm.shape
    t = SEQ_BLOCK
    nb = seq // t + 1
    skip = lp // t - nb
    prev_rows = max(skip * t // 8 - 1, 0)
    return pl.pallas_call(
        functools.partial(_mlstm_kernel, skip > 0),
        grid=(b, nb),
        in_specs=[pl.BlockSpec((None, t, wz), lambda i, j: (i, j + skip, 0)),
                  pl.BlockSpec((None, t, GATE_LANES), lambda i, j: (i, j + skip, 0)),
                  pl.BlockSpec((None, 16, t), lambda i, j: (i, 0, j + skip)),
                  pl.BlockSpec((None, 8, GATE_LANES), lambda i, j: (i, prev_rows, 0)),
                  pl.BlockSpec(convw.shape, lambda i, j: (0, 0)),
                  pl.BlockSpec(gain.shape, lambda i, j: (0, 0))],
        out_specs=pl.BlockSpec((None, t, M_WIDTH), lambda i, j: (i, jnp.maximum(j - 1, 0), 0)),
        out_shape=jax.ShapeDtypeStruct((b, seq, M_WIDTH), BF16),
        scratch_shapes=[pltpu.VMEM((t, 2 * M_WIDTH), F32),
                        pltpu.VMEM((M_HEADS, M_HEAD_DIM, M_HEAD_DIM), F32),
                        pltpu.VMEM((M_HEADS, 8, M_HEAD_DIM), F32),
                        pltpu.VMEM((M_HEADS, 8, 128), F32),
                        pltpu.VMEM((8, GATE_LANES), F32)],
        compiler_params=_params("parallel", "arbitrary"),
        name="mlstm",
    )(zm, gcol, grow, gcol, convw, gain)


_FOX_KDIM = 128
_LOG2E = 1.4426950408889634


def _split3(c):
    c0 = c.astype(BF16).astype(F32)
    r1 = c - c0
    c1 = r1.astype(BF16).astype(F32)
    c2 = (r1 - c1).astype(BF16).astype(F32)
    return c0, c1, c2


def _fox_kernel(first_real, q_ref, k_ref, v_ref, gcol_ref, y_ref,
                kx_ref, vt_ref, qx_ref, *state_refs):
    qi = pl.program_id(1) + 1
    t = FOX_BLOCK
    d = F_HEAD_DIM
    lp = k_ref.shape[0]
    elane = lax.broadcasted_iota(jnp.int32, (t, _FOX_KDIM - d), 1)

    @pl.when(qi == 1)
    def _():
        def prep(c, carry):
            r0 = pl.multiple_of(c * t, t)
            rows = pl.ds(r0, t)
            vt_ref[:, rows] = v_ref[:, rows].astype(BF16)
            g = gcol_ref[rows, :]
            for h in range(F_HEADS):
                kn = k_ref[rows, h * d:(h + 1) * d]
                c0, c1, c2 = _split3(g[:, 2 * M_HEADS + h:2 * M_HEADS + h + 1] * _LOG2E)
                extra = jnp.where(elane < 3, 1.0,
                                  jnp.where(elane == 3, -c0, jnp.where(elane == 4, -c1,
                                                                       jnp.where(elane == 5, -c2, 0.0))))
                kx_ref[rows, h * _FOX_KDIM:(h + 1) * _FOX_KDIM] = jnp.concatenate([kn, extra], axis=1).astype(BF16)
            return carry
        lax.fori_loop(0, lp // t, prep, 0)

    scale = d ** -0.5 * _LOG2E
    gq_rows = gcol_ref[pl.ds(pl.multiple_of(qi * t, t), t), :] * _LOG2E
    for h in range(F_HEADS):
        qn = q_ref[:, h * d:(h + 1) * d] * scale
        c0, c1, c2 = _split3(gq_rows[:, 2 * M_HEADS + h:2 * M_HEADS + h + 1])
        extra = jnp.where(elane == 0, c0, jnp.where(elane == 1, c1, jnp.where(elane == 2, c2,
                                                                              jnp.where(elane < 6, 1.0, 0.0))))
        qx_ref[h] = jnp.concatenate([qn, extra], axis=1).astype(BF16)

    acc_refs, m_refs, l_refs = (state_refs[i * F_HEADS:(i + 1) * F_HEADS] for i in range(3))
    for h in range(F_HEADS):
        m_refs[h][...] = jnp.full_like(m_refs[h], NEG_BIG)
        l_refs[h][...] = jnp.zeros_like(l_refs[h])
        acc_refs[h][...] = jnp.zeros_like(acc_refs[h])

    kk = lax.broadcasted_iota(jnp.int32, (t, t), 0)
    qq = lax.broadcasted_iota(jnp.int32, (t, t), 1)

    def step(jb, mask):
        start = pl.multiple_of(jb * t, t)
        scores = []
        for h in range(F_HEADS):
            kx = kx_ref[pl.ds(start, t), h * _FOX_KDIM:(h + 1) * _FOX_KDIM]
            scores.append(lax.dot_general(kx, qx_ref[h], _NT, preferred_element_type=F32))
        probs, alphas = [], []
        for h in range(F_HEADS):
            s = scores[h] if mask is None else jnp.where(mask, scores[h], NEG_BIG)
            m_old = m_refs[h][...]
            m_new = jnp.maximum(m_old, jnp.max(s, axis=0, keepdims=True))
            p = jnp.exp2(s - m_new)
            alpha = jnp.exp2(m_old - m_new)
            l_refs[h][...] = alpha * l_refs[h][...] + jnp.sum(p, axis=0, keepdims=True)
            m_refs[h][...] = m_new
            probs.append(p.astype(BF16))
            alphas.append(alpha)
        for h in range(F_HEADS):
            acc_refs[h][...] = alphas[h] * acc_refs[h][...] + _dot(vt_ref[h * d:(h + 1) * d, pl.ds(start, t)],
                                                                  probs[h])

    step(0, kk >= first_real)
    lax.fori_loop(1, qi, lambda jb, c: (step(jb, None), c)[1], 0)
    step(qi, kk <= qq)
    yt = jnp.concatenate([acc_refs[h][...] / l_refs[h][...] for h in range(F_HEADS)], axis=0)
    y_ref[...] = yt.T.astype(y_ref.dtype)


def _fox(zf, vt, gcol, first_real):
    b, lp, _ = zf.shape
    t = FOX_BLOCK
    nb = lp // t
    return pl.pallas_call(
        functools.partial(_fox_kernel, first_real),
        grid=(b, nb - 1),
        in_specs=[pl.BlockSpec((None, t, F_WIDTH), lambda i, j: (i, j + 1, 0)),
                  pl.BlockSpec((None, lp, F_WIDTH), lambda i, j: (i, 0, 1)),
                  pl.BlockSpec((F_WIDTH, lp), lambda i, j: (0, i)),
                  pl.BlockSpec((None, lp, GATE_LANES), lambda i, j: (i, 0, 0))],
        out_specs=pl.BlockSpec((None, t, F_WIDTH), lambda i, j: (i, j, 0)),
        out_shape=jax.ShapeDtypeStruct((b, lp - t, F_WIDTH), BF16),
        scratch_shapes=[pltpu.VMEM((lp, F_HEADS * _FOX_KDIM), BF16),
                        pltpu.VMEM((F_WIDTH, lp), BF16),
                        pltpu.VMEM((F_HEADS, t, _FOX_KDIM), BF16)]
                       + [pltpu.VMEM((F_HEAD_DIM, t), F32)] * F_HEADS
                       + [pltpu.VMEM((1, t), F32)] * F_HEADS
                       + [pltpu.VMEM((1, t), F32)] * F_HEADS,
        compiler_params=_params("parallel", "arbitrary"),
        name="fox",
    )(zf, zf, vt, gcol)


def _outproj_kernel(ym_ref, yf_ref, x_ref, wm_ref, wf_ref, g_ref, h2_ref, xn_ref):
    h2 = x_ref[...] + (_dot(ym_ref[...], wm_ref[...]) + _dot(yf_ref[...], wf_ref[...]))
    h2_ref[...] = h2
    xn = h2 * lax.rsqrt(jnp.mean(h2 * h2, axis=-1, keepdims=True) + EPS) * g_ref[...]
    xn_ref[...] = pltpu.bitcast(xn.astype(BF16), jnp.uint32)


def _outproj(ym, yf, x2d, wm, wf, g, tm):
    n, d = x2d.shape
    row = lambda w: pl.BlockSpec((tm, w), lambda i: (i, 0))
    full = lambda a: pl.BlockSpec(a.shape, lambda i: (0, 0))
    return pl.pallas_call(
        _outproj_kernel,
        grid=(n // tm,),
        in_specs=[row(ym.shape[1]), row(yf.shape[1]), row(d), full(wm), full(wf), full(g)],
        out_specs=[row(d), pl.BlockSpec((tm // 2, d), lambda i: (i, 0))],
        out_shape=[jax.ShapeDtypeStruct((n, d), F32),
                   jax.ShapeDtypeStruct((n // 2, d), jnp.uint32)],
        compiler_params=_params("parallel"),
        name="outproj",
    )(ym, yf, x2d, wm, wf, g)


_TOP = PEER_TOPK + 1
_TOP_ROWS = 24


def _merge_exchange_network(n):
    t = max(1, (n - 1).bit_length())
    pairs = []
    p = 1 << (t - 1)
    while p > 0:
        q, r, d = 1 << (t - 1), 0, p
        while d > 0:
            pairs += [(i, i + d) for i in range(n - d) if (i & p) == r]
            d, q, r = q - p, q >> 1, p
        p >>= 1
    return pairs


def _pop_largest(s):
    m = s.shape[0] // 8
    cols = [s[8 * i:8 * (i + 1), :] for i in range(m)]
    for i, j in _merge_exchange_network(m):
        cols[i], cols[j] = jnp.maximum(cols[i], cols[j]), jnp.minimum(cols[i], cols[j])
    values, counts = [], []
    for it in range(_TOP):
        best = jnp.max(cols[0], axis=0, keepdims=True)
        eq = cols[0] == best
        values.append(best)
        counts.append(jnp.sum(jnp.where(eq, 1.0, 0.0), axis=0, keepdims=True))
        live = min(m, _TOP - it)
        for d in range(live - 1):
            cols[d] = jnp.where(eq, cols[d + 1], cols[d])
        if live == m:
            cols[m - 1] = jnp.where(eq, -jnp.inf, cols[m - 1])
    return values, counts


def _top_values(s):
    tb = s.shape[1]
    rank = lax.broadcasted_iota(jnp.int32, (_TOP_ROWS, tb), 0).astype(F32)
    vals = None
    cnt = jnp.zeros((1, tb), F32)
    for best, k in zip(*_pop_largest(s)):
        vals = jnp.where(rank >= cnt, best, -jnp.inf if vals is None else vals)
        cnt = cnt + k
    return jnp.where(rank < _TOP, vals, -jnp.inf)


def _pair_threshold(a, b):
    k = PEER_TOPK
    slabs = [a[0:1, :] + b[0:_TOP_ROWS, :]]
    for r in range(1, 8):
        slabs.append(a[r:r + 1, :] + b[0:8, :])
    slabs.append(a[8:_TOP_ROWS, :] + b[0:1, :])
    top = a[0:1, :] + b[0:1, :]
    cnt = jnp.zeros_like(top)
    v_k = top
    v_k1 = top
    z = jnp.zeros_like(top)
    for best, n_eq in zip(*_pop_largest(jnp.concatenate(slabs, axis=0))):
        take = jnp.clip(k - cnt, 0.0, n_eq)
        z = z + take * jnp.exp(best - top)
        v_k = jnp.where(cnt < k, best, v_k)
        v_k1 = jnp.where(cnt < k + 1, best, v_k1)
        cnt = cnt + n_eq
    return 0.5 * (v_k + v_k1), top, z


def _count_leading(sorted_rows, pred):
    out = None
    for c in range(_TOP):
        hit = pred(sorted_rows[c:c + 1, :])
        out = jnp.where(hit, c + 1.0, 0.0 if out is None else out)
    return out


def _peer_route_kernel(xn_ref, wq_ref, kh_ref, kl_ref, r2_ref, e2_ref, n_ref, e1_ref):
    q = _dot(pltpu.bitcast(xn_ref[...], BF16), wq_ref[...])
    for h in range(PEER_HEADS):
        st = []
        for p in range(2):
            i = 2 * h + p
            qh, ql = _split_bf16(q[:, i * PEER_HALF:(i + 1) * PEER_HALF])
            st.append(_dot3(kh_ref[i], kl_ref[i], qh, ql, _NT))
        a = _top_values(st[0])
        b = _top_values(st[1])
        thr, top, z = _pair_threshold(a, b)
        need = thr - st[0]
        above2 = _count_leading(b, lambda row: row > st[1])
        n1 = _count_leading(b, lambda row: row >= need)
        r2_ref[h] = pltpu.bitcast(above2.astype(BF16), jnp.uint32)
        e2_ref[h] = pltpu.bitcast(jnp.exp(st[1] - b[0:1, :]).astype(BF16), jnp.uint32)
        n_ref[h] = n1
        e1_ref[h] = jnp.exp(st[0] - a[0:1, :]) * (0.5 / z)


def _peer_route(xn, wq, kh, kl, tb):
    n, d = 2 * xn.shape[0], xn.shape[1]
    row = pl.BlockSpec((tb // 2, d), lambda i: (i, 0))
    full2 = lambda a: pl.BlockSpec(a.shape, lambda i: (0, 0))
    full3 = lambda a: pl.BlockSpec(a.shape, lambda i: (0, 0, 0))
    out = lambda rows: pl.BlockSpec((PEER_HEADS, rows, tb), lambda i: (0, 0, i))
    shape = lambda rows, dt: jax.ShapeDtypeStruct((PEER_HEADS, rows, n), dt)
    return pl.pallas_call(
        _peer_route_kernel,
        grid=(n // tb,),
        in_specs=[row, full2(wq), full3(kh), full3(kl)],
        out_specs=[out(N_KEYS // 2), out(N_KEYS // 2), out(N_KEYS), out(N_KEYS)],
        out_shape=[shape(N_KEYS // 2, jnp.uint32), shape(N_KEYS // 2, jnp.uint32),
                   shape(N_KEYS, F32), shape(N_KEYS, F32)],
        compiler_params=_params("parallel"),
        name="peer_route",
    )(xn, wq, kh, kl)


def _twice_gelu(x):
    return x * (1.0 + lax.erf(x * (2.0 ** -0.5)))


_PACK = 16


_ROWS_PER_PIECE = 4
_PIECES_PER_CHUNK = 1


def _pack_rows_kernel(x_ref, o_ref):
    o_ref[...] = pltpu.bitcast(x_ref[...].astype(BF16), jnp.uint32)


def _pack_rows_t_kernel(x_ref, o_ref):
    o_ref[...] = pltpu.bitcast(x_ref[...].T.astype(BF16), jnp.uint32)


def _pack_bf16(x, transpose, rows):
    n, c = x.shape
    if transpose:
        kern, out_shape, out_spec = _pack_rows_t_kernel, (c // 2, n), pl.BlockSpec((c // 2, rows), lambda i: (0, i))
    else:
        kern, out_shape, out_spec = _pack_rows_kernel, (n // 2, c), pl.BlockSpec((rows // 2, c), lambda i: (i, 0))
    return pl.pallas_call(
        kern,
        grid=(n // rows,),
        in_specs=[pl.BlockSpec((rows, c), lambda i: (i, 0))],
        out_specs=out_spec,
        out_shape=jax.ShapeDtypeStruct(out_shape, jnp.uint32),
        compiler_params=_params("parallel"),
        name="pack_t" if transpose else "pack",
    )(x)


def _peer_expert_kernel(ib, xn_ref, u_ref, vt_ref, r2_ref, e2_ref, n_ref, e1_ref, h2_ref, o_ref,
                        acc_ref, w_ref, act_ref):
    e = pl.program_id(1)
    tb = 2 * xn_ref.shape[0]
    d = 2 * vt_ref.shape[0]
    piece = _ROWS_PER_PIECE * N_KEYS
    chunk = _PIECES_PER_CHUNK * piece
    n_chunks = ib * N_KEYS // chunk
    out_rows = d // _PIECES_PER_CHUNK

    @pl.when(e == 0)
    def _():
        acc_ref[...] = jnp.zeros_like(acc_ref)

    def act_piece(c, k):
        lo = c * chunk + k * piece
        act_ref[c % 2, k * piece:(k + 1) * piece, :] = lax.dot_general(
            pltpu.bitcast(u_ref[lo // 2:(lo + piece) // 2, :], BF16), pltpu.bitcast(xn_ref[...], BF16), _NT,
            preferred_element_type=F32)

    def out_piece(c, k):
        rows = slice(k * out_rows, (k + 1) * out_rows)
        vt = pltpu.bitcast(vt_ref[k * out_rows // 2:(k + 1) * out_rows // 2, c * chunk:(c + 1) * chunk], BF16)
        acc_ref[rows, :] += _dot(vt, w_ref[c * chunk:(c + 1) * chunk, :])

    def gate_piece(c, k):
        i0 = (c * chunk + k * piece) // N_KEYS
        for tc in range(tb // 128):
            cols = slice(tc * 128, (tc + 1) * 128)
            bcast = lambda ref, h, i: jnp.broadcast_to(ref[h, i:i + 1, cols], (_PACK, 128)).astype(BF16)
            gates = [[None] * (N_KEYS // _PACK) for _ in range(_ROWS_PER_PIECE)]
            for h in range(PEER_HEADS):
                n1 = [bcast(n_ref, h, i0 + r) for r in range(_ROWS_PER_PIECE)]
                e1 = [bcast(e1_ref, h, i0 + r) for r in range(_ROWS_PER_PIECE)]
                for g in range(N_KEYS // _PACK):
                    ks = slice(g * _PACK // 2, (g + 1) * _PACK // 2)
                    above2 = pltpu.bitcast(r2_ref[h, ks, cols], BF16)
                    e2 = pltpu.bitcast(e2_ref[h, ks, cols], BF16)
                    for r in range(_ROWS_PER_PIECE):
                        term = e1[r] * jnp.where(above2 < n1[r], e2, jnp.zeros_like(e2))
                        gates[r][g] = term if gates[r][g] is None else gates[r][g] + term
            for r in range(_ROWS_PER_PIECE):
                for g in range(N_KEYS // _PACK):
                    lo = k * piece + r * N_KEYS + g * _PACK
                    act = act_ref[c % 2, lo:lo + _PACK, cols]
                    w_ref[c * chunk + lo:c * chunk + lo + _PACK, cols] = gates[r][g] * _twice_gelu(act).astype(BF16)

    for k in range(_PIECES_PER_CHUNK):
        act_piece(0, k)
    for c in range(n_chunks):
        for k in range(_PIECES_PER_CHUNK):
            if c + 1 < n_chunks:
                act_piece(c + 1, k)
            if c > 0:
                out_piece(c - 1, k)
            gate_piece(c, k)
    for k in range(_PIECES_PER_CHUNK):
        out_piece(n_chunks - 1, k)

    @pl.when(e == pl.num_programs(1) - 1)
    def _():
        o_ref[...] = h2_ref[...] + acc_ref[...].T


def _peer_experts(xn, u_packed, vt_packed, r2, e2, n1, e1, h2, tb, ib):
    n, d = 2 * xn.shape[0], xn.shape[1]
    ne = 2 * u_packed.shape[0]
    eb = ib * N_KEYS
    chunk = _PIECES_PER_CHUNK * _ROWS_PER_PIECE * N_KEYS
    assert eb % chunk == 0 and tb % 128 == 0
    gate_full = pl.BlockSpec((PEER_HEADS, N_KEYS // 2, tb), lambda t, e: (0, 0, t))
    gate_rows = pl.BlockSpec((PEER_HEADS, ib, tb), lambda t, e: (0, e, t))
    return pl.pallas_call(
        functools.partial(_peer_expert_kernel, ib),
        grid=(n // tb, ne // eb),
        in_specs=[pl.BlockSpec((tb // 2, d), lambda t, e: (t, 0)),
                  pl.BlockSpec((eb // 2, d), lambda t, e: (e, 0)),
                  pl.BlockSpec((d // 2, eb), lambda t, e: (0, e)),
                  gate_full, gate_full, gate_rows, gate_rows,
                  pl.BlockSpec((tb, d), lambda t, e: (t, 0))],
        out_specs=pl.BlockSpec((tb, d), lambda t, e: (t, 0)),
        out_shape=jax.ShapeDtypeStruct((n, d), F32),
        scratch_shapes=[pltpu.VMEM((d, tb), F32),
                        pltpu.VMEM((eb, tb), BF16),
                        pltpu.VMEM((2, chunk, tb), F32)],
        compiler_params=_params("parallel", "arbitrary"),
        name="peer_experts",
    )(xn, u_packed, vt_packed, r2, e2, n1, e1, h2)


def _tile(n, prefer):
    for t in prefer:
        if n % t == 0:
            return t
    raise ValueError(f"no tile for {n}")


def kernel(x, meta_tokens, norm_mix, w_in, conv_qk, b_igate, b_fgate_m, m_out_norm, b_fgate_f,
           f_q_norm, f_k_norm, w_out, norm_ffn, peer_query, peer_sub_keys, peer_u, peer_v):
    assert w_in.shape[0] == 1, "single-layer block"
    b, seq, d = x.shape
    t = FOX_BLOCK
    assert seq % t == 0 and N_META <= SEQ_BLOCK and t % SEQ_BLOCK == 0
    first_real = t - N_META

    head = jnp.concatenate([jnp.zeros((first_real, d), x.dtype), meta_tokens.astype(x.dtype)], axis=0)
    w = w_in[0]
    o = 0
    cols = {}
    for name, size in (("mq", M_WIDTH), ("mk", M_WIDTH), ("mv", M_WIDTH), ("mo", M_WIDTH), ("mi", M_HEADS),
                       ("mf", M_HEADS), ("fq", F_WIDTH), ("fk", F_WIDTH), ("fv", F_WIDTH), ("ff", F_HEADS)):
        cols[name] = w[:, o:o + size]
        o += size
    w_m = jnp.concatenate([cols["mq"], cols["mk"], cols["mv"], cols["mo"]], axis=1).astype(BF16)
    w_f = jnp.concatenate([cols["fq"], cols["fk"]], axis=1).astype(BF16)
    w_vt = cols["fv"].T.astype(BF16)
    n_gate = 2 * M_HEADS + F_HEADS
    w_g = jnp.concatenate([cols["mi"], cols["mf"], cols["ff"], jnp.zeros((d, GATE_LANES - n_gate), F32)], axis=1)
    w_gh, w_gl = _split_bf16(w_g)
    gate_bias = jnp.concatenate([b_igate[0], b_fgate_m[0], b_fgate_f[0],
                                 jnp.zeros((GATE_LANES - n_gate,), F32)]).reshape(1, GATE_LANES)

    f_gain = jnp.concatenate([jnp.tile(f_q_norm[0], F_HEADS), jnp.tile(f_k_norm[0], F_HEADS)]).reshape(1, 2 * F_WIDTH)
    zm, zf, vt, zg = _inproj(x, head, norm_mix[0].reshape(1, d), w_m, w_f, f_gain, w_vt, w_gh, w_gl)
    gcol, grow = _gates(zg, gate_bias)
    y_m = _mlstm(zm, gcol, grow, conv_qk[0], m_out_norm[0].reshape(1, M_WIDTH), seq)
    y_f = _fox(zf, vt, gcol, first_real)
    n = b * seq
    wo = w_out[0].astype(BF16)
    h2, xh = _outproj(y_m.reshape(n, M_WIDTH), y_f.reshape(n, F_WIDTH), x.reshape(n, d),
                      wo[:M_WIDTH], wo[M_WIDTH:], norm_ffn[0].reshape(1, d), _tile(n, (512, 256, 128)))

    kh, kl = _split_bf16(peer_sub_keys[0].reshape(2 * PEER_HEADS, N_KEYS, PEER_HALF))
    r2, e2, n1, e1 = _peer_route(xh, peer_query[0].astype(BF16), kh, kl, _tile(n, (256, 128)))
    u_packed = _pack_bf16(peer_u[0], False, 1024)
    vt_packed = _pack_bf16(peer_v[0], True, 1024)
    out = _peer_experts(xh, u_packed, vt_packed, r2, e2, n1, e1, h2, _tile(n, (512, 256, 128)), 16)
    return out.reshape(b, seq, d)
```

```python
import functools

import jax
import jax.numpy as jnp
from jax import lax
from jax.experimental import pallas as pl
from jax.experimental.pallas import tpu as pltpu

F32 = jnp.float32
BF16 = jnp.bfloat16

EPS = 1e-6
N_META = 16
M_HEADS = 4
M_HEAD_DIM = 128
M_WIDTH = M_HEADS * M_HEAD_DIM
CONV_W = 4
F_HEADS = 8
F_HEAD_DIM = 64
F_WIDTH = F_HEADS * F_HEAD_DIM
PEER_HEADS = 8
N_KEYS = 128
PEER_TOPK = 16
PEER_HALF = 128

SEQ_BLOCK = 128
FOX_BLOCK = 256
GATE_LANES = 128
NEG_BIG = -1e30
VMEM_LIMIT = 56 * 1024 * 1024

_NT = (((1,), (1,)), ((), ()))
_TN = (((0,), (0,)), ((), ()))


def _params(*sem, flags=None):
    return pltpu.CompilerParams(dimension_semantics=sem, vmem_limit_bytes=VMEM_LIMIT, flags=flags)


def _split_bf16(a):
    hi = a.astype(BF16)
    lo = (a - hi.astype(F32)).astype(BF16)
    return hi, lo


def _dot(a, b):
    return jnp.dot(a, b, preferred_element_type=F32)


def _dot3(ah, al, bh, bl, dims=None):
    if dims is None:
        f = _dot
    else:
        f = lambda x, y: lax.dot_general(x, y, dims, preferred_element_type=F32)
    return f(ah, bh) + (f(ah, bl) + f(al, bh))


def _inproj_kernel(x_ref, head_ref, g_ref, wm_ref, wf_ref, fg_ref, wvt_ref, wgh_ref, wgl_ref,
                   zm_ref, zf_ref, vt_ref, zg_ref):
    h = jnp.where(pl.program_id(1) == 0, head_ref[...], x_ref[...])
    hn = h * lax.rsqrt(jnp.mean(h * h, axis=-1, keepdims=True) + EPS) * g_ref[...]
    hb, hl = _split_bf16(hn)
    zm_ref[...] = _dot(hb, wm_ref[...])
    zf = _dot(hb, wf_ref[...])
    low = lax.broadcasted_iota(jnp.int32, (zf.shape[0], 128), 1) < F_HEAD_DIM
    for c in range(zf.shape[1] // 128):
        x = zf[:, c * 128:(c + 1) * 128]
        x2 = x * x
        ms_lo = jnp.sum(jnp.where(low, x2, 0.0), axis=1, keepdims=True) * (1.0 / F_HEAD_DIM)
        ms_hi = jnp.sum(jnp.where(low, 0.0, x2), axis=1, keepdims=True) * (1.0 / F_HEAD_DIM)
        r = jnp.where(low, lax.rsqrt(ms_lo + EPS), lax.rsqrt(ms_hi + EPS))
        zf_ref[:, c * 128:(c + 1) * 128] = x * r * fg_ref[:, c * 128:(c + 1) * 128]
    vt_ref[...] = lax.dot_general(wvt_ref[...], hb, _NT, preferred_element_type=F32)
    zg_ref[...] = _dot3(hb, hl, wgh_ref[...], wgl_ref[...])


def _inproj(x, head, g, wm, wf, fg, wvt, wgh, wgl):
    b, seq, d = x.shape
    tm = head.shape[0]
    nb = seq // tm + 1
    lp = nb * tm
    full = lambda a: pl.BlockSpec(a.shape, lambda i, j: (0, 0))
    rows = lambda w: pl.BlockSpec((None, tm, w), lambda i, j: (i, j, 0))
    return pl.pallas_call(
        _inproj_kernel,
        grid=(b, nb),
        in_specs=[pl.BlockSpec((None, tm, d), lambda i, j: (i, jnp.maximum(j - 1, 0), 0)), full(head), full(g),
                  full(wm), full(wf), full(fg), full(wvt), full(wgh), full(wgl)],
        out_specs=[rows(wm.shape[1]), rows(wf.shape[1]),
                   pl.BlockSpec((wvt.shape[0], tm), lambda i, j: (0, i * nb + j)),
                   rows(GATE_LANES)],
        out_shape=[jax.ShapeDtypeStruct((b, lp, wm.shape[1]), F32),
                   jax.ShapeDtypeStruct((b, lp, wf.shape[1]), F32),
                   jax.ShapeDtypeStruct((wvt.shape[0], b * lp), F32),
                   jax.ShapeDtypeStruct((b, lp, GATE_LANES), F32)],
        compiler_params=_params("parallel", "arbitrary"),
        name="inproj",
    )(x, head, g, wm, wf, fg, wvt, wgh, wgl)


def _log_sigmoid(x):
    return jnp.minimum(x, 0.0) - jnp.log1p(jnp.exp(-jnp.abs(x)))


def _gates_kernel(zg_ref, bias_ref, gcol_ref, grow_ref):
    lp = zg_ref.shape[0]
    t = SEQ_BLOCK
    r = lax.broadcasted_iota(jnp.int32, (t, t), 0)
    c = lax.broadcasted_iota(jnp.int32, (t, t), 1)
    tri = jnp.where(r >= c, 1.0, 0.0).astype(BF16)
    lane = lax.broadcasted_iota(jnp.int32, (t, GATE_LANES), 1)
    carry = jnp.zeros((1, GATE_LANES), F32)
    for i in range(lp // t):
        pre = zg_ref[i * t:(i + 1) * t, :] + bias_ref[...]
        ls = _log_sigmoid(pre)
        l0 = ls.astype(BF16)
        r1 = ls - l0.astype(F32)
        l1 = r1.astype(BF16)
        l2 = (r1 - l1.astype(F32)).astype(BF16)
        cs = (_dot(tri, l0) + (_dot(tri, l1) + _dot(tri, l2))) + carry
        carry = cs[t - 1:t, :]
        out = jnp.where(lane < M_HEADS, pre, cs)
        gcol_ref[i * t:(i + 1) * t, :] = out
        grow_ref[:, i * t:(i + 1) * t] = out.T[:16, :]


def _gates(zg, bias):
    b, lp, _ = zg.shape
    return pl.pallas_call(
        _gates_kernel,
        grid=(b,),
        in_specs=[pl.BlockSpec((None, lp, GATE_LANES), lambda i: (i, 0, 0)),
                  pl.BlockSpec((1, GATE_LANES), lambda i: (0, 0))],
        out_specs=[pl.BlockSpec((None, lp, GATE_LANES), lambda i: (i, 0, 0)),
                   pl.BlockSpec((None, 16, lp), lambda i: (i, 0, 0))],
        out_shape=[jax.ShapeDtypeStruct((b, lp, GATE_LANES), F32),
                   jax.ShapeDtypeStruct((b, 16, lp), F32)],
        compiler_params=_params("parallel"),
        name="gates",
    )(zg, bias)


def _mlstm_kernel(has_prev, zm_ref, gcol_ref, grow_ref, gprev_ref, convw_ref, gain_ref, y_ref,
                  prevx_ref, ct_ref, n_ref, m_ref, pc_ref):
    j = pl.program_id(1)
    t = SEQ_BLOCK
    w2 = 2 * M_WIDTH

    @pl.when(j == 0)
    def _():
        prevx_ref[...] = jnp.zeros_like(prevx_ref)
        ct_ref[...] = jnp.zeros_like(ct_ref)
        n_ref[...] = jnp.zeros_like(n_ref)
        m_ref[...] = jnp.zeros_like(m_ref)
        pc_ref[...] = (jnp.broadcast_to(gprev_ref[7:8, :], pc_ref.shape) if has_prev
                       else jnp.zeros_like(pc_ref))

    xqk = zm_ref[:, :w2]
    prev = prevx_ref[...]
    rowi = lax.broadcasted_iota(jnp.int32, (t, w2), 0)
    acc = xqk * convw_ref[CONV_W - 1:CONV_W, :]
    for s in range(1, CONV_W):
        shifted = jnp.where(rowi < s, pltpu.roll(prev, s, 0), pltpu.roll(xqk, s, 0))
        acc = acc + shifted * convw_ref[CONV_W - 1 - s:CONV_W - s, :]
    prevx_ref[...] = xqk
    qk = acc * jax.nn.sigmoid(acc)

    gcol = gcol_ref[...]
    grow = grow_ref[...]
    rr = lax.broadcasted_iota(jnp.int32, (t, t), 0)
    cc = lax.broadcasted_iota(jnp.int32, (t, t), 1)
    causal = rr >= cc
    scale = M_HEAD_DIM ** -0.5

    heads = range(M_HEADS)
    hsl = [slice(h * M_HEAD_DIM, (h + 1) * M_HEAD_DIM) for h in heads]
    q = [qk[:, hsl[h]] for h in heads]
    k = [qk[:, M_WIDTH + h * M_HEAD_DIM:M_WIDTH + (h + 1) * M_HEAD_DIM] * scale for h in heads]
    qb = [x.astype(BF16) for x in q]
    kb = [x.astype(BF16) for x in k]
    vb = [zm_ref[:, w2 + h * M_HEAD_DIM:w2 + (h + 1) * M_HEAD_DIM].astype(BF16) for h in heads]
    ct = [ct_ref[h] for h in heads]
    n_row = [n_ref[h, 0:1, :] for h in heads]
    m_prev = [m_ref[h, 0:1, 0:1] for h in heads]

    qk_dot = [lax.dot_general(qb[h], kb[h], _NT, preferred_element_type=F32) for h in heads]
    q_ct = [_dot(qb[h], ct[h].astype(BF16)) for h in heads]

    li_c = [gcol[:, h:h + 1] for h in heads]
    li_r = [grow[h:h + 1, :] for h in heads]
    prev_cum = [pc_ref[0:1, M_HEADS + h:M_HEADS + h + 1] for h in heads]
    b_c = [gcol[:, M_HEADS + h:M_HEADS + h + 1] - prev_cum[h] for h in heads]
    b_r = [grow[M_HEADS + h:M_HEADS + h + 1, :] - prev_cum[h] for h in heads]
    dm = [jnp.where(causal, (b_c[h] - b_r[h]) + li_r[h], -jnp.inf) for h in heads]
    m_inter = [b_c[h] + m_prev[h] for h in heads]
    m_t = [jnp.maximum(m_inter[h], jnp.max(dm[h], axis=1, keepdims=True)) for h in heads]
    w_inter = [jnp.exp(m_inter[h] - m_t[h]) for h in heads]
    s_mat = [qk_dot[h] * jnp.exp(dm[h] - m_t[h]) for h in heads]

    num = [_dot(s_mat[h].astype(BF16), vb[h]) + w_inter[h] * q_ct[h] for h in heads]
    den = [jnp.sum(s_mat[h], axis=1, keepdims=True)
           + w_inter[h] * jnp.sum(q[h] * n_row[h], axis=1, keepdims=True) for h in heads]
    hh = [num[h] / jnp.maximum(jnp.abs(den[h]), jnp.exp(-m_t[h])) for h in heads]

    b_end = [b_c[h][t - 1:t, :] for h in heads]
    g_c = [(b_end[h] - b_c[h]) + li_c[h] for h in heads]
    g_r = [(b_end[h] - b_r[h]) + li_r[h] for h in heads]
    m_new = [jnp.maximum(b_end[h] + m_prev[h], jnp.max(g_r[h], axis=1, keepdims=True)) for h in heads]
    decay = [jnp.exp(b_end[h] + m_prev[h] - m_new[h]) for h in heads]
    kw = [k[h] * jnp.exp(g_c[h] - m_new[h]) for h in heads]
    for h in heads:
        ct_ref[h] = decay[h] * ct[h] + lax.dot_general(kw[h].astype(BF16), vb[h], _TN, preferred_element_type=F32)
        n_ref[h] = jnp.broadcast_to(decay[h] * n_row[h] + jnp.sum(kw[h], axis=0, keepdims=True), (8, M_HEAD_DIM))
        m_ref[h] = jnp.broadcast_to(m_new[h], (8, 128))

    for h in heads:
        o_pre = zm_ref[:, w2 + M_WIDTH + h * M_HEAD_DIM:w2 + M_WIDTH + (h + 1) * M_HEAD_DIM]
        hn = hh[h] * lax.rsqrt(jnp.mean(hh[h] * hh[h], axis=-1, keepdims=True) + EPS) * gain_ref[:, hsl[h]]
        y_ref[:, hsl[h]] = (hn * jax.nn.sigmoid(o_pre)).astype(y_ref.dtype)

    pc_ref[...] = jnp.broadcast_to(gcol[t - 1:t, :], (8, GATE_LANES))


def _mlstm(zm, gcol, grow, convw, gain, seq):
    b, lp, wz = zm.shape
    t = SEQ_BLOCK
    nb = seq // t + 1
    skip = lp // t - nb
    prev_rows = max(skip * t // 8 - 1, 0)
    return pl.pallas_call(
        functools.partial(_mlstm_kernel, skip > 0),
        grid=(b, nb),
        in_specs=[pl.BlockSpec((None, t, wz), lambda i, j: (i, j + skip, 0)),
                  pl.BlockSpec((None, t, GATE_LANES), lambda i, j: (i, j + skip, 0)),
                  pl.BlockSpec((None, 16, t), lambda i, j: (i, 0, j + skip)),
                  pl.BlockSpec((None, 8, GATE_LANES), lambda i, j: (i, prev_rows, 0)),
                  pl.BlockSpec(convw.shape, lambda i, j: (0, 0)),
                  pl.BlockSpec(gain.shape, lambda i, j: (0, 0))],
        out_specs=pl.BlockSpec((None, t, M_WIDTH), lambda i, j: (i, jnp.maximum(j - 1, 0), 0)),
        out_shape=jax.ShapeDtypeStruct((b, seq, M_WIDTH), BF16),
        scratch_shapes=[pltpu.VMEM((t, 2 * M_WIDTH), F32),
                        pltpu.VMEM((M_HEADS, M_HEAD_DIM, M_HEAD_DIM), F32),
                        pltpu.VMEM((M_HEADS, 8, M_HEAD_DIM), F32),
                        pltpu.VMEM((M_HEADS, 8, 128), F32),
                        pltpu.VMEM((8, GATE_LANES), F32)],
        compiler_params=_params("parallel", "arbitrary"),
        name="mlstm",
    )(zm, gcol, grow, gcol, convw, gain)


_FOX_KDIM = 128
_LOG2E = 1.4426950408889634


def _split3(c):
    c0 = c.astype(BF16).astype(F32)
    r1 = c - c0
    c1 = r1.astype(BF16).astype(F32)
    c2 = (r1 - c1).astype(BF16).astype(F32)
    return c0, c1, c2


def _fox_kernel(first_real, q_ref, k_ref, v_ref, gcol_ref, y_ref,
                kx_ref, vt_ref, qx_ref, *state_refs):
    qi = pl.program_id(1) + 1
    t = FOX_BLOCK
    d = F_HEAD_DIM
    lp = k_ref.shape[0]
    elane = lax.broadcasted_iota(jnp.int32, (t, _FOX_KDIM - d), 1)

    @pl.when(qi == 1)
    def _():
        def prep(c, carry):
            r0 = pl.multiple_of(c * t, t)
            rows = pl.ds(r0, t)
            vt_ref[:, rows] = v_ref[:, rows].astype(BF16)
            g = gcol_ref[rows, :]
            for h in range(F_HEADS):
                kn = k_ref[rows, h * d:(h + 1) * d]
                c0, c1, c2 = _split3(g[:, 2 * M_HEADS + h:2 * M_HEADS + h + 1] * _LOG2E)
                extra = jnp.where(elane < 3, 1.0,
                                  jnp.where(elane == 3, -c0, jnp.where(elane == 4, -c1,
                                                                       jnp.where(elane == 5, -c2, 0.0))))
                kx_ref[rows, h * _FOX_KDIM:(h + 1) * _FOX_KDIM] = jnp.concatenate([kn, extra], axis=1).astype(BF16)
            return carry
        lax.fori_loop(0, lp // t, prep, 0)

    scale = d ** -0.5 * _LOG2E
    gq_rows = gcol_ref[pl.ds(pl.multiple_of(qi * t, t), t), :] * _LOG2E
    for h in range(F_HEADS):
        qn = q_ref[:, h * d:(h + 1) * d] * scale
        c0, c1, c2 = _split3(gq_rows[:, 2 * M_HEADS + h:2 * M_HEADS + h + 1])
        extra = jnp.where(elane == 0, c0, jnp.where(elane == 1, c1, jnp.where(elane == 2, c2,
                                                                              jnp.where(elane < 6, 1.0, 0.0))))
        qx_ref[h] = jnp.concatenate([qn, extra], axis=1).astype(BF16)

    acc_refs, m_refs, l_refs = (state_refs[i * F_HEADS:(i + 1) * F_HEADS] for i in range(3))
    for h in range(F_HEADS):
        m_refs[h][...] = jnp.full_like(m_refs[h], NEG_BIG)
        l_refs[h][...] = jnp.zeros_like(l_refs[h])
        acc_refs[h][...] = jnp.zeros_like(acc_refs[h])

    kk = lax.broadcasted_iota(jnp.int32, (t, t), 0)
    qq = lax.broadcasted_iota(jnp.int32, (t, t), 1)

    def step(jb, mask):
        start = pl.multiple_of(jb * t, t)
        scores = []
        for h in range(F_HEADS):
            kx = kx_ref[pl.ds(start, t), h * _FOX_KDIM:(h + 1) * _FOX_KDIM]
            scores.append(lax.dot_general(kx, qx_ref[h], _NT, preferred_element_type=F32))
        probs, alphas = [], []
        for h in range(F_HEADS):
            s = scores[h] if mask is None else jnp.where(mask, scores[h], NEG_BIG)
            m_old = m_refs[h][...]
            m_new = jnp.maximum(m_old, jnp.max(s, axis=0, keepdims=True))
            p = jnp.exp2(s - m_new)
            alpha = jnp.exp2(m_old - m_new)
            l_refs[h][...] = alpha * l_refs[h][...] + jnp.sum(p, axis=0, keepdims=True)
            m_refs[h][...] = m_new
            probs.append(p.astype(BF16))
            alphas.append(alpha)
        for h in range(F_HEADS):
            acc_refs[h][...] = alphas[h] * acc_refs[h][...] + _dot(vt_ref[h * d:(h + 1) * d, pl.ds(start, t)],
                                                                  probs[h])

    step(0, kk >= first_real)
    lax.fori_loop(1, qi, lambda jb, c: (step(jb, None), c)[1], 0)
    step(qi, kk <= qq)
    yt = jnp.concatenate([acc_refs[h][...] / l_refs[h][...] for h in range(F_HEADS)], axis=0)
    y_ref[...] = yt.T.astype(y_ref.dtype)


def _fox(zf, vt, gcol, first_real):
    b, lp, _ = zf.shape
    t = FOX_BLOCK
    nb = lp // t
    return pl.pallas_call(
        functools.partial(_fox_kernel, first_real),
        grid=(b, nb - 1),
        in_specs=[pl.BlockSpec((None, t, F_WIDTH), lambda i, j: (i, j + 1, 0)),
                  pl.BlockSpec((None, lp, F_WIDTH), lambda i, j: (i, 0, 1)),
                  pl.BlockSpec((F_WIDTH, lp), lambda i, j: (0, i)),
                  pl.BlockSpec((None, lp, GATE_LANES), lambda i, j: (i, 0, 0))],
        out_specs=pl.BlockSpec((None, t, F_WIDTH), lambda i, j: (i, j, 0)),
        out_shape=jax.ShapeDtypeStruct((b, lp - t, F_WIDTH), BF16),
        scratch_shapes=[pltpu.VMEM((lp, F_HEADS * _FOX_KDIM), BF16),
                        pltpu.VMEM((F_WIDTH, lp), BF16),
                        pltpu.VMEM((F_HEADS, t, _FOX_KDIM), BF16)]
                       + [pltpu.VMEM((F_HEAD_DIM, t), F32)] * F_HEADS
                       + [pltpu.VMEM((1, t), F32)] * F_HEADS
                       + [pltpu.VMEM((1, t), F32)] * F_HEADS,
        compiler_params=_params("parallel", "arbitrary"),
        name="fox",
    )(zf, zf, vt, gcol)


def _outproj_kernel(ym_ref, yf_ref, x_ref, wm_ref, wf_ref, g_ref, h2_ref, xn_ref):
    h2 = x_ref[...] + (_dot(ym_ref[...], wm_ref[...]) + _dot(yf_ref[...], wf_ref[...]))
    h2_ref[...] = h2
    xn = h2 * lax.rsqrt(jnp.mean(h2 * h2, axis=-1, keepdims=True) + EPS) * g_ref[...]
    xn_ref[...] = pltpu.bitcast(xn.astype(BF16), jnp.uint32)


def _outproj(ym, yf, x2d, wm, wf, g, tm):
    n, d = x2d.shape
    row = lambda w: pl.BlockSpec((tm, w), lambda i: (i, 0))
    full = lambda a: pl.BlockSpec(a.shape, lambda i: (0, 0))
    return pl.pallas_call(
        _outproj_kernel,
        grid=(n // tm,),
        in_specs=[row(ym.shape[1]), row(yf.shape[1]), row(d), full(wm), full(wf), full(g)],
        out_specs=[row(d), pl.BlockSpec((tm // 2, d), lambda i: (i, 0))],
        out_shape=[jax.ShapeDtypeStruct((n, d), F32),
                   jax.ShapeDtypeStruct((n // 2, d), jnp.uint32)],
        compiler_params=_params("parallel"),
        name="outproj",
    )(ym, yf, x2d, wm, wf, g)


_TOP = PEER_TOPK + 1
_TOP_ROWS = 24


def _merge_exchange_network(n):
    t = max(1, (n - 1).bit_length())
    pairs = []
    p = 1 << (t - 1)
    while p > 0:
        q, r, d = 1 << (t - 1), 0, p
        while d > 0:
            pairs += [(i, i + d) for i in range(n - d) if (i & p) == r]
            d, q, r = q - p, q >> 1, p
        p >>= 1
    return pairs


def _pop_largest(s):
    m = s.shape[0] // 8
    cols = [s[8 * i:8 * (i + 1), :] for i in range(m)]
    for i, j in _merge_exchange_network(m):
        cols[i], cols[j] = jnp.maximum(cols[i], cols[j]), jnp.minimum(cols[i], cols[j])
    values, counts = [], []
    for it in range(_TOP):
        best = jnp.max(cols[0], axis=0, keepdims=True)
        eq = cols[0] == best
        values.append(best)
        counts.append(jnp.sum(jnp.where(eq, 1.0, 0.0), axis=0, keepdims=True))
        live = min(m, _TOP - it)
        for d in range(live - 1):
            cols[d] = jnp.where(eq, cols[d + 1], cols[d])
        if live == m:
            cols[m - 1] = jnp.where(eq, -jnp.inf, cols[m - 1])
    return values, counts


def _top_values(s):
    tb = s.shape[1]
    rank = lax.broadcasted_iota(jnp.int32, (_TOP_ROWS, tb), 0).astype(F32)
    vals = None
    cnt = jnp.zeros((1, tb), F32)
    for best, k in zip(*_pop_largest(s)):
        vals = jnp.where(rank >= cnt, best, -jnp.inf if vals is None else vals)
        cnt = cnt + k
    return jnp.where(rank < _TOP, vals, -jnp.inf)


def _pair_threshold(a, b):
    k = PEER_TOPK
    slabs = [a[0:1, :] + b[0:_TOP_ROWS, :]]
    for r in range(1, 8):
        slabs.append(a[r:r + 1, :] + b[0:8, :])
    slabs.append(a[8:_TOP_ROWS, :] + b[0:1, :])
    top = a[0:1, :] + b[0:1, :]
    cnt = jnp.zeros_like(top)
    v_k = top
    v_k1 = top
    z = jnp.zeros_like(top)
    for best, n_eq in zip(*_pop_largest(jnp.concatenate(slabs, axis=0))):
        take = jnp.clip(k - cnt, 0.0, n_eq)
        z = z + take * jnp.exp(best - top)
        v_k = jnp.where(cnt < k, best, v_k)
        v_k1 = jnp.where(cnt < k + 1, best, v_k1)
        cnt = cnt + n_eq
    return 0.5 * (v_k + v_k1), top, z


def _count_leading(sorted_rows, pred):
    out = None
    for c in range(_TOP):
        hit = pred(sorted_rows[c:c + 1, :])
        out = jnp.where(hit, c + 1.0, 0.0 if out is None else out)
    return out


def _peer_route_kernel(xn_ref, wq_ref, kh_ref, kl_ref, r2_ref, e2_ref, n_ref, e1_ref):
    q = _dot(pltpu.bitcast(xn_ref[...], BF16), pltpu.bitcast(wq_ref[...], BF16))
    for h in range(PEER_HEADS):
        st = []
        for p in range(2):
            i = 2 * h + p
            qh, ql = _split_bf16(q[:, i * PEER_HALF:(i + 1) * PEER_HALF])
            st.append(_dot3(kh_ref[i], kl_ref[i], qh, ql, _NT))
        a = _top_values(st[0])
        b = _top_values(st[1])
        thr, top, z = _pair_threshold(a, b)
        need = thr - st[0]
        above2 = _count_leading(b, lambda row: row > st[1])
        n1 = _count_leading(b, lambda row: row >= need)
        r2_ref[h] = pltpu.bitcast(above2.astype(BF16), jnp.uint32)
        e2_ref[h] = pltpu.bitcast(jnp.exp(st[1] - b[0:1, :]).astype(BF16), jnp.uint32)
        n_ref[h] = n1
        e1_ref[h] = jnp.exp(st[0] - a[0:1, :]) * (0.5 / z)


def _peer_route(xn, wq, kh, kl, tb):
    n, d = 2 * xn.shape[0], xn.shape[1]
    row = pl.BlockSpec((tb // 2, d), lambda i: (i, 0))
    full2 = lambda a: pl.BlockSpec(a.shape, lambda i: (0, 0))
    full3 = lambda a: pl.BlockSpec(a.shape, lambda i: (0, 0, 0))
    out = lambda rows: pl.BlockSpec((PEER_HEADS, rows, tb), lambda i: (0, 0, i))
    shape = lambda rows, dt: jax.ShapeDtypeStruct((PEER_HEADS, rows, n), dt)
    return pl.pallas_call(
        _peer_route_kernel,
        grid=(n // tb,),
        in_specs=[row, full2(wq), full3(kh), full3(kl)],
        out_specs=[out(N_KEYS // 2), out(N_KEYS // 2), out(N_KEYS), out(N_KEYS)],
        out_shape=[shape(N_KEYS // 2, jnp.uint32), shape(N_KEYS // 2, jnp.uint32),
                   shape(N_KEYS, F32), shape(N_KEYS, F32)],
        compiler_params=_params("parallel"),
        name="peer_route",
    )(xn, wq, kh, kl)


def _twice_gelu(x):
    return x * (1.0 + lax.erf(x * (2.0 ** -0.5)))


_PACK = 16


_ROWS_PER_PIECE = 4
_PIECES_PER_CHUNK = 1


def _pack_rows_kernel(x_ref, o_ref):
    o_ref[...] = pltpu.bitcast(x_ref[...].astype(BF16), jnp.uint32)


def _pack_rows_t_kernel(x_ref, o_ref):
    o_ref[...] = pltpu.bitcast(x_ref[...].T.astype(BF16), jnp.uint32)


def _pack_bf16(x, transpose, rows):
    n, c = x.shape
    if transpose:
        kern, out_shape, out_spec = _pack_rows_t_kernel, (c // 2, n), pl.BlockSpec((c // 2, rows), lambda i: (0, i))
    else:
        kern, out_shape, out_spec = _pack_rows_kernel, (n // 2, c), pl.BlockSpec((rows // 2, c), lambda i: (i, 0))
    return pl.pallas_call(
        kern,
        grid=(n // rows,),
        in_specs=[pl.BlockSpec((rows, c), lambda i: (i, 0))],
        out_specs=out_spec,
        out_shape=jax.ShapeDtypeStruct(out_shape, jnp.uint32),
        compiler_params=_params("parallel"),
        name="pack_t" if transpose else "pack",
    )(x)


def _peer_expert_kernel(ib, xn_ref, u_ref, vt_ref, r2_ref, e2_ref, n_ref, e1_ref, h2_ref, o_ref,
                        acc_ref, w_ref, act_ref):
    e = pl.program_id(1)
    tb = 2 * xn_ref.shape[0]
    d = 2 * vt_ref.shape[0]
    piece = _ROWS_PER_PIECE * N_KEYS
    chunk = _PIECES_PER_CHUNK * piece
    n_chunks = ib * N_KEYS // chunk
    out_rows = d // _PIECES_PER_CHUNK

    @pl.when(e == 0)
    def _():
        acc_ref[...] = jnp.zeros_like(acc_ref)

    def act_piece(c, k):
        lo = c * chunk + k * piece
        act_ref[c % 2, k * piece:(k + 1) * piece, :] = lax.dot_general(
            pltpu.bitcast(u_ref[lo // 2:(lo + piece) // 2, :], BF16), pltpu.bitcast(xn_ref[...], BF16), _NT,
            preferred_element_type=F32)

    def out_piece(c, k):
        rows = slice(k * out_rows, (k + 1) * out_rows)
        vt = pltpu.bitcast(vt_ref[k * out_rows // 2:(k + 1) * out_rows // 2, c * chunk:(c + 1) * chunk], BF16)
        acc_ref[rows, :] += _dot(vt, w_ref[c * chunk:(c + 1) * chunk, :])

    def gate_piece(c, k):
        i0 = (c * chunk + k * piece) // N_KEYS
        for tc in range(tb // 128):
            cols = slice(tc * 128, (tc + 1) * 128)
            bcast = lambda ref, h, i: jnp.broadcast_to(ref[h, i:i + 1, cols], (_PACK, 128)).astype(BF16)
            gates = [[None] * (N_KEYS // _PACK) for _ in range(_ROWS_PER_PIECE)]
            for h in range(PEER_HEADS):
                n1 = [bcast(n_ref, h, i0 + r) for r in range(_ROWS_PER_PIECE)]
                e1 = [bcast(e1_ref, h, i0 + r) for r in range(_ROWS_PER_PIECE)]
                for g in range(N_KEYS // _PACK):
                    ks = slice(g * _PACK // 2, (g + 1) * _PACK // 2)
                    above2 = pltpu.bitcast(r2_ref[h, ks, cols], BF16)
                    e2 = pltpu.bitcast(e2_ref[h, ks, cols], BF16)
                    for r in range(_ROWS_PER_PIECE):
                        term = e1[r] * jnp.where(above2 < n1[r], e2, jnp.zeros_like(e2))
                        gates[r][g] = term if gates[r][g] is None else gates[r][g] + term
            for r in range(_ROWS_PER_PIECE):
                for g in range(N_KEYS // _PACK):
                    lo = k * piece + r * N_KEYS + g * _PACK
                    act = act_ref[c % 2, lo:lo + _PACK, cols]
                    w_ref[c * chunk + lo:c * chunk + lo + _PACK, cols] = gates[r][g] * _twice_gelu(act).astype(BF16)

    for k in range(_PIECES_PER_CHUNK):
        act_piece(0, k)
    for c in range(n_chunks):
        for k in range(_PIECES_PER_CHUNK):
            if c + 1 < n_chunks:
                act_piece(c + 1, k)
            if c > 0:
                out_piece(c - 1, k)
            gate_piece(c, k)
    for k in range(_PIECES_PER_CHUNK):
        out_piece(n_chunks - 1, k)

    @pl.when(e == pl.num_programs(1) - 1)
    def _():
        o_ref[...] = h2_ref[...] + acc_ref[...].T


def _peer_experts(xn, u_packed, vt_packed, r2, e2, n1, e1, h2, tb, ib):
    n, d = 2 * xn.shape[0], xn.shape[1]
    ne = 2 * u_packed.shape[0]
    eb = ib * N_KEYS
    chunk = _PIECES_PER_CHUNK * _ROWS_PER_PIECE * N_KEYS
    assert eb % chunk == 0 and tb % 128 == 0
    gate_full = pl.BlockSpec((PEER_HEADS, N_KEYS // 2, tb), lambda t, e: (0, 0, t))
    gate_rows = pl.BlockSpec((PEER_HEADS, ib, tb), lambda t, e: (0, e, t))
    return pl.pallas_call(
        functools.partial(_peer_expert_kernel, ib),
        grid=(n // tb, ne // eb),
        in_specs=[pl.BlockSpec((tb // 2, d), lambda t, e: (t, 0)),
                  pl.BlockSpec((eb // 2, d), lambda t, e: (e, 0)),
                  pl.BlockSpec((d // 2, eb), lambda t, e: (0, e)),
                  gate_full, gate_full, gate_rows, gate_rows,
                  pl.BlockSpec((tb, d), lambda t, e: (t, 0))],
        out_specs=pl.BlockSpec((tb, d), lambda t, e: (t, 0)),
        out_shape=jax.ShapeDtypeStruct((n, d), F32),
        scratch_shapes=[pltpu.VMEM((d, tb), F32),
                        pltpu.VMEM((eb, tb), BF16),
                        pltpu.VMEM((2, chunk, tb), F32)],
        compiler_params=_params("parallel", "arbitrary"),
        name="peer_experts",
    )(xn, u_packed, vt_packed, r2, e2, n1, e1, h2)


def _tile(n, prefer):
    for t in prefer:
        if n % t == 0:
            return t
    raise ValueError(f"no tile for {n}")


def kernel(x, meta_tokens, norm_mix, w_in, conv_qk, b_igate, b_fgate_m, m_out_norm, b_fgate_f,
           f_q_norm, f_k_norm, w_out, norm_ffn, peer_query, peer_sub_keys, peer_u, peer_v):
    assert w_in.shape[0] == 1, "single-layer block"
    b, seq, d = x.shape
    t = FOX_BLOCK
    assert seq % t == 0 and N_META <= SEQ_BLOCK and t % SEQ_BLOCK == 0
    first_real = t - N_META

    head = jnp.concatenate([jnp.zeros((first_real, d), x.dtype), meta_tokens.astype(x.dtype)], axis=0)
    w = w_in[0]
    o = 0
    cols = {}
    for name, size in (("mq", M_WIDTH), ("mk", M_WIDTH), ("mv", M_WIDTH), ("mo", M_WIDTH), ("mi", M_HEADS),
                       ("mf", M_HEADS), ("fq", F_WIDTH), ("fk", F_WIDTH), ("fv", F_WIDTH), ("ff", F_HEADS)):
        cols[name] = w[:, o:o + size]
        o += size
    w_m = jnp.concatenate([cols["mq"], cols["mk"], cols["mv"], cols["mo"]], axis=1).astype(BF16)
    w_f = jnp.concatenate([cols["fq"], cols["fk"]], axis=1).astype(BF16)
    w_vt = cols["fv"].T.astype(BF16)
    n_gate = 2 * M_HEADS + F_HEADS
    w_g = jnp.concatenate([cols["mi"], cols["mf"], cols["ff"], jnp.zeros((d, GATE_LANES - n_gate), F32)], axis=1)
    w_gh, w_gl = _split_bf16(w_g)
    gate_bias = jnp.concatenate([b_igate[0], b_fgate_m[0], b_fgate_f[0],
                                 jnp.zeros((GATE_LANES - n_gate,), F32)]).reshape(1, GATE_LANES)

    f_gain = jnp.concatenate([jnp.tile(f_q_norm[0], F_HEADS), jnp.tile(f_k_norm[0], F_HEADS)]).reshape(1, 2 * F_WIDTH)
    zm, zf, vt, zg = _inproj(x, head, norm_mix[0].reshape(1, d), w_m, w_f, f_gain, w_vt, w_gh, w_gl)
    gcol, grow = _gates(zg, gate_bias)
    y_m = _mlstm(zm, gcol, grow, conv_qk[0], m_out_norm[0].reshape(1, M_WIDTH), seq)
    y_f = _fox(zf, vt, gcol, first_real)
    n = b * seq
    wo = w_out[0].astype(BF16)
    h2, xh = _outproj(y_m.reshape(n, M_WIDTH), y_f.reshape(n, F_WIDTH), x.reshape(n, d),
                      wo[:M_WIDTH], wo[M_WIDTH:], norm_ffn[0].reshape(1, d), _tile(n, (512, 256, 128)))

    kh, kl = _split_bf16(peer_sub_keys[0].reshape(2 * PEER_HEADS, N_KEYS, PEER_HALF))
    r2, e2, n1, e1 = _peer_route(xh, _pack_bf16(peer_query[0], False, 256), kh, kl, _tile(n, (256, 128)))
    u_packed = _pack_bf16(peer_u[0], False, 1024)
    vt_packed = _pack_bf16(peer_v[0], True, 1024)
    out = _peer_experts(xh, u_packed, vt_packed, r2, e2, n1, e1, h2, _tile(n, (512, 256, 128)), 16)
    return out.reshape(b, seq, d)
```
